```python
import jax, jax.numpy as jnp
from jax import lax
import numpy as np

D_MODEL = 2048
BATCH = 2
SEQ = 8192
DEPTH = 4

N_MIXERS = 3
N_FOX = (DEPTH + 2) // 3
N_RWKV = (DEPTH + 1) // 3
N_CONV = DEPTH // 3

FOX_HEADS = 16
FOX_HEAD_DIM = D_MODEL // FOX_HEADS
FOX_BLOCK = 128
FORGET_BIAS = 2.0
FOX_PROJ = 4 * D_MODEL + FOX_HEADS

RWKV_HEAD_DIM = 64
RWKV_HEADS = D_MODEL // RWKV_HEAD_DIM
DECAY_LORA = 96
ICLR_LORA = 96
GATE_LORA = 256
RWKV_GN_EPS = 64e-5
RWKV_PROJ = 3 * D_MODEL + DECAY_LORA + ICLR_LORA + GATE_LORA

CONV_WIDTH = 31

MEM_TOKENS = 256
MEM_HEADS = 4
MEM_HEAD_DIM = D_MODEL // MEM_HEADS

D_FF = 5632
FFN_CONV_WIDTH = 3

RMS_EPS = 1e-6
LN_EPS = 1e-5

kernel_name = 'hybrid_fox_rwkv7_conformer_trunk'


def rms_norm(x, g):
    xf = x.astype(jnp.float32)
    y = xf * lax.rsqrt(jnp.mean(xf * xf, axis=-1, keepdims=True) + RMS_EPS)
    return (y * g.astype(jnp.float32)).astype(x.dtype)


def layer_norm(x, g, b):
    xf = x.astype(jnp.float32)
    mu = jnp.mean(xf, axis=-1, keepdims=True)
    var = jnp.mean(jnp.square(xf - mu), axis=-1, keepdims=True)
    y = (xf - mu) * lax.rsqrt(var + LN_EPS)
    return (y * g.astype(jnp.float32) + b.astype(jnp.float32)).astype(x.dtype)


def causal_depthwise_conv(x, w, b):
    K, C = w.shape
    y = lax.conv_general_dilated(
        x, w[:, None, :].astype(x.dtype), window_strides=(1,), padding=[(K - 1, 0)],
        dimension_numbers=('NWC', 'WIO', 'NWC'), feature_group_count=C)
    return y + b


def forgetting_attention(h, w_in, b_f, q_gain, k_gain, w_o):
    B, S, D = h.shape
    H, Dh, L = FOX_HEADS, FOX_HEAD_DIM, FOX_BLOCK
    nb = S // L
    q, k, v, gate, f_logit = jnp.split(h @ w_in, [D, 2 * D, 3 * D, 4 * D], axis=-1)
    q = rms_norm(q.reshape(B, S, H, Dh), q_gain)
    k = rms_norm(k.reshape(B, S, H, Dh), k_gain)
    v = v.reshape(B, S, H, Dh)
    log_f = jax.nn.log_sigmoid((f_logit + b_f).astype(jnp.float32))
    c = jnp.cumsum(log_f, axis=1)
    kt = k.transpose(0, 2, 1, 3)
    vt = v.transpose(0, 2, 1, 3)
    ck = c.transpose(0, 2, 1)
    qb = q.reshape(B, nb, L, H, Dh).transpose(1, 0, 3, 2, 4)
    cq = c.reshape(B, nb, L, H).transpose(1, 0, 3, 2)
    kpos = jnp.arange(S)
    scale = Dh ** -0.5

    def query_block(args):
        q_i, c_i, blk = args
        qpos = blk * L + jnp.arange(L)
        s = jnp.einsum('bhqd,bhkd->bhqk', q_i, kt).astype(jnp.float32) * scale
        s = s + (c_i[..., :, None] - ck[..., None, :])
        s = jnp.where(kpos[None, :] <= qpos[:, None], s, -jnp.inf)
        p = jax.nn.softmax(s, axis=-1)
        return jnp.einsum('bhqk,bhkd->bhqd', p.astype(vt.dtype), vt)

    o = lax.map(query_block, (qb, cq, jnp.arange(nb)))
    o = o.transpose(1, 0, 3, 2, 4).reshape(B, S, D)
    return (o * jax.nn.sigmoid(gate)) @ w_o


def rwkv7_scan(r, w, k, v, kk, a):
    B, S, H, N = r.shape

    def step(state, inp):
        r_t, w_t, k_t, v_t, kk_t, a_t = inp
        sa = jnp.einsum('bhvk,bhk->bhv', state, -kk_t)
        state = (state * w_t[:, :, None, :]
                 + sa[..., None] * (kk_t * a_t)[:, :, None, :]
                 + v_t[..., None] * k_t[:, :, None, :])
        return state, jnp.einsum('bhvk,bhk->bhv', state, r_t)

    xs = tuple(t.transpose(1, 0, 2, 3) for t in (r, w, k, v, kk, a))
    _, y = lax.scan(step, jnp.zeros((B, H, N, N), jnp.float32), xs)
    return y.transpose(1, 0, 2, 3)


def rwkv7_time_mix(h, w_in, mu, w0, decay_b, a0, iclr_b, gate_b, k_k, k_a, r_k, ln_w, ln_b, w_o):
    B, S, D = h.shape
    H, N = RWKV_HEADS, RWKV_HEAD_DIM
    proj = h @ w_in
    prev = jnp.pad(proj, ((0, 0), (1, 0), (0, 0)))[:, :-1]
    proj = proj + (prev - proj) * mu
    r, k, v, w_lo, a_lo, g_lo = jnp.split(
        proj, [D, 2 * D, 3 * D, 3 * D + DECAY_LORA, 3 * D + DECAY_LORA + ICLR_LORA], axis=-1)
    w_log = -jax.nn.softplus(-(w0 + jnp.tanh(w_lo) @ decay_b)) - 0.5
    decay = jnp.exp(-jnp.exp(w_log.astype(jnp.float32)))
    a = jax.nn.sigmoid(a0 + a_lo @ iclr_b)
    g = jax.nn.sigmoid(g_lo) @ gate_b
    heads = lambda t: t.reshape(B, S, H, N).astype(jnp.float32)
    kk = heads(k * k_k)
    kk = kk / jnp.maximum(jnp.sqrt(jnp.sum(kk * kk, axis=-1, keepdims=True)), 1e-12)
    k = k * (1 + (a - 1) * k_a)
    rf, kf, vf = heads(r), heads(k), heads(v)
    y = rwkv7_scan(rf, heads(decay), kf, vf, kk, heads(a))
    mean = jnp.mean(y, axis=-1, keepdims=True)
    var = jnp.mean(jnp.square(y - mean), axis=-1, keepdims=True)
    y = ((y - mean) * lax.rsqrt(var + RWKV_GN_EPS)).reshape(B, S, D)
    y = y * ln_w.astype(jnp.float32) + ln_b.astype(jnp.float32)
    bonus = jnp.sum(rf * kf * r_k.astype(jnp.float32), axis=-1, keepdims=True) * vf
    y = (y + bonus.reshape(B, S, D)).astype(h.dtype)
    return (y * g) @ w_o


def conformer_conv_module(h, w_in, dw_w, dw_b, ln_g, ln_b, w_o):
    u, gate = jnp.split(h @ w_in, 2, axis=-1)
    z = u * jax.nn.sigmoid(gate)
    z = causal_depthwise_conv(z, dw_w, dw_b)
    z = jax.nn.silu(layer_norm(z, ln_g, ln_b))
    return z @ w_o


def memory_cross_attention(h, mem_n, w_q, w_kv, w_o):
    B, S, D = h.shape
    M = mem_n.shape[1]
    q = (h @ w_q).reshape(B, S, MEM_HEADS, MEM_HEAD_DIM)
    k, v = jnp.split(mem_n @ w_kv, 2, axis=-1)
    k = k.reshape(B, M, MEM_HEADS, MEM_HEAD_DIM)
    v = v.reshape(B, M, MEM_HEADS, MEM_HEAD_DIM)
    s = jnp.einsum('bqhd,bmhd->bhqm', q, k).astype(jnp.float32) * (MEM_HEAD_DIM ** -0.5)
    p = jax.nn.softmax(s, axis=-1)
    o = jnp.einsum('bhqm,bmhd->bqhd', p.astype(v.dtype), v).reshape(B, S, D)
    return o @ w_o


def conv_gated_mlp(h, w_up, dw_w, dw_b, w_down):
    up = causal_depthwise_conv(h @ w_up, dw_w, dw_b)
    u, g = jnp.split(up, 2, axis=-1)
    return (jax.nn.silu(g) * u) @ w_down


def setup_inputs(seed: int = 0) -> dict:
    key = jax.random.key(seed)
    ks = iter(jax.random.split(key, 48))
    f32 = jnp.float32
    D = D_MODEL

    def nrm(shape, scale):
        return jax.random.normal(next(ks), shape, f32) * scale

    def gain(shape):
        return 1.0 + nrm(shape, 0.02)

    return {
        'x': nrm((BATCH, SEQ, D), 1.0),
        'mem': nrm((BATCH, MEM_TOKENS, D), 1.0),
        'mem_norm': gain((D,)),
        'norm_mix': gain((DEPTH, D)),
        'norm_cross': gain((DEPTH, D)),
        'norm_ffn': gain((DEPTH, D)),
        'final_norm': gain((D,)),
        'fox_w_in': nrm((N_FOX, D, FOX_PROJ), D ** -0.5),
        'fox_b_f': FORGET_BIAS + nrm((N_FOX, FOX_HEADS), 0.5),
        'fox_q_gain': gain((N_FOX, FOX_HEAD_DIM)),
        'fox_k_gain': gain((N_FOX, FOX_HEAD_DIM)),
        'fox_w_o': nrm((N_FOX, D, D), D ** -0.5),
        'rwkv_w_in': nrm((N_RWKV, D, RWKV_PROJ), D ** -0.5),
        'rwkv_mu': jax.random.uniform(next(ks), (N_RWKV, RWKV_PROJ), f32),
        'rwkv_w0': nrm((N_RWKV, D), 0.5),
        'rwkv_decay_b': nrm((N_RWKV, DECAY_LORA, D), DECAY_LORA ** -0.5),
        'rwkv_a0': nrm((N_RWKV, D), 0.1),
        'rwkv_iclr_b': nrm((N_RWKV, ICLR_LORA, D), ICLR_LORA ** -0.5),
        'rwkv_gate_b': nrm((N_RWKV, GATE_LORA, D), GATE_LORA ** -0.5),
        'rwkv_k_k': 0.85 + nrm((N_RWKV, D), 0.05),
        'rwkv_k_a': gain((N_RWKV, D)),
        'rwkv_r_k': nrm((N_RWKV, RWKV_HEADS, RWKV_HEAD_DIM), 0.1),
        'rwkv_ln_w': gain((N_RWKV, D)),
        'rwkv_ln_b': nrm((N_RWKV, D), 0.02),
        'rwkv_w_o': nrm((N_RWKV, D, D), D ** -0.5),
        'conv_w_in': nrm((N_CONV, D, 2 * D), D ** -0.5),
        'conv_dw_w': nrm((N_CONV, CONV_WIDTH, D), CONV_WIDTH ** -0.5),
        'conv_dw_b': nrm((N_CONV, D), 0.02),
        'conv_ln_g': gain((N_CONV, D)),
        'conv_ln_b': nrm((N_CONV, D), 0.02),
        'conv_w_o': nrm((N_CONV, D, D), D ** -0.5),
        'cross_w_q': nrm((DEPTH, D, D), D ** -0.5),
        'cross_w_kv': nrm((DEPTH, D, 2 * D), D ** -0.5),
        'cross_w_o': nrm((DEPTH, D, D), D ** -0.5),
        'ffn_w_up': nrm((DEPTH, D, 2 * D_FF), D ** -0.5),
        'ffn_dw_w': nrm((DEPTH, FFN_CONV_WIDTH, 2 * D_FF), FFN_CONV_WIDTH ** -0.5),
        'ffn_dw_b': nrm((DEPTH, 2 * D_FF), 0.02),
        'ffn_w_down': nrm((DEPTH, D_FF, D), D_FF ** -0.5),
    }


def reference(x, mem, mem_norm, norm_mix, norm_cross, norm_ffn, final_norm,
              fox_w_in, fox_b_f, fox_q_gain, fox_k_gain, fox_w_o,
              rwkv_w_in, rwkv_mu, rwkv_w0, rwkv_decay_b, rwkv_a0, rwkv_iclr_b, rwkv_gate_b,
              rwkv_k_k, rwkv_k_a, rwkv_r_k, rwkv_ln_w, rwkv_ln_b, rwkv_w_o,
              conv_w_in, conv_dw_w, conv_dw_b, conv_ln_g, conv_ln_b, conv_w_o,
              cross_w_q, cross_w_kv, cross_w_o,
              ffn_w_up, ffn_dw_w, ffn_dw_b, ffn_w_down):
    mem_n = rms_norm(mem, mem_norm)
    for i in range(DEPTH):
        kind = i % N_MIXERS
        j = i // N_MIXERS
        h = rms_norm(x, norm_mix[i])
        if kind == 0:
            y = forgetting_attention(h, fox_w_in[j], fox_b_f[j], fox_q_gain[j], fox_k_gain[j], fox_w_o[j])
        elif kind == 1:
            y = rwkv7_time_mix(h, rwkv_w_in[j], rwkv_mu[j], rwkv_w0[j], rwkv_decay_b[j], rwkv_a0[j],
                               rwkv_iclr_b[j], rwkv_gate_b[j], rwkv_k_k[j], rwkv_k_a[j], rwkv_r_k[j],
                               rwkv_ln_w[j], rwkv_ln_b[j], rwkv_w_o[j])
        else:
            y = conformer_conv_module(h, conv_w_in[j], conv_dw_w[j], conv_dw_b[j],
                                      conv_ln_g[j], conv_ln_b[j], conv_w_o[j])
        x = x + y
        x = x + memory_cross_attention(rms_norm(x, norm_cross[i]), mem_n,
                                       cross_w_q[i], cross_w_kv[i], cross_w_o[i])
        x = x + conv_gated_mlp(rms_norm(x, norm_ffn[i]), ffn_w_up[i], ffn_dw_w[i], ffn_dw_b[i], ffn_w_down[i])
    return rms_norm(x, final_norm)
```

```python
import functools

import jax
import jax.numpy as jnp
from jax import lax
from jax.experimental import pallas as pl
from jax.experimental.pallas import tpu as pltpu

F32 = jnp.float32
BF16 = jnp.bfloat16

RMS_EPS = 1e-6
LN_EPS = 1e-5
RWKV_GN_EPS = 64e-5

LANES = 128
HALO = 16
VMEM_LIMIT = 56 * 1024 * 1024

FOX_HEADS = 16
FOX_HEAD_DIM = 128
MEM_HEADS = 4
RWKV_HEAD_DIM = 64
RWKV_CHUNK = 64
DECAY_LORA = 96
ICLR_LORA = 96
GATE_LORA = 256
CONV_HALO = 32
NEG_BIG = -1e30


def _cparams(*sem):
    return pltpu.CompilerParams(dimension_semantics=sem, vmem_limit_bytes=VMEM_LIMIT)


def _rms_rows(x, g):
    ms = jnp.mean(x * x, axis=-1, keepdims=True)
    return x * lax.rsqrt(ms + RMS_EPS) * g


def _sigmoid(x):
    return 1.0 / (1.0 + jnp.exp(-x))


def _dot(a, b):
    return jnp.dot(a, b, preferred_element_type=F32)


def _dot_nt(a, b):
    return lax.dot_general(a, b, (((1,), (1,)), ((), ())), preferred_element_type=F32)


def _dot_tn(a, b):
    return lax.dot_general(a, b, (((0,), (0,)), ((), ())), preferred_element_type=F32)


def _split3(x):
    hi = x.astype(BF16)
    r1 = x - hi.astype(F32)
    mid = r1.astype(BF16)
    lo = (r1 - mid.astype(F32)).astype(BF16)
    return hi, mid, lo


def _dot_exact_lhs(sel, x):
    hi, mid, lo = _split3(x)
    return _dot(sel, hi) + _dot(sel, mid) + _dot(sel, lo)


def _dot_exact_rhs(x, sel):
    hi, mid, lo = _split3(x)
    return _dot(hi, sel) + _dot(mid, sel) + _dot(lo, sel)


def _proj_body(*refs, n_w, halo, epilogue, tm, tn, tiles_per_seq, scale, n_cols):
    it = iter(refs)
    x_ref = next(it)
    xp_ref = next(it) if halo else None
    g_ref = next(it)
    w_refs = [next(it) for _ in range(n_w)]
    if epilogue == "ffn":
        cw_refs = [next(it) for _ in range(n_w)]
        cb_refs = [next(it) for _ in range(n_w)]
    elif epilogue == "lerp":
        mu_ref = next(it)
    elif epilogue == "fox":
        gain_ref = next(it)
    out_ref = next(it)
    h_scr = next(it)
    acc_scrs = [next(it) for _ in range(n_w)] if halo else None

    m = pl.program_id(0)
    n = pl.program_id(1)
    off = HALO if halo else 0

    @pl.when(n == 0)
    def _():
        g = g_ref[...]
        h_scr[pl.ds(off, tm), :] = _rms_rows(x_ref[...], g).astype(BF16)
        if halo:
            hp = _rms_rows(xp_ref[...], g)
            first = (m % tiles_per_seq) == 0
            h_scr[pl.ds(0, HALO), :] = jnp.where(first, 0.0, hp).astype(BF16)

    h = h_scr[...]
    accs = [_dot(h, w_ref[...]) for w_ref in w_refs]

    if epilogue == "scale_bf16":
        out_ref[...] = (accs[0] * scale).astype(out_ref.dtype)
    elif epilogue == "glu":
        out_ref[...] = (accs[0] * _sigmoid(accs[1])).astype(out_ref.dtype)
    elif epilogue == "fox":
        nqk = 2 * n_cols // 4 // tn
        nv = 3 * n_cols // 4 // tn
        acc = accs[0]

        @pl.when(n < nqk)
        def _():
            gain = gain_ref[...]
            for j in range(tn // FOX_HEAD_DIM):
                sl = slice(j * FOX_HEAD_DIM, (j + 1) * FOX_HEAD_DIM)
                out_ref[:, sl] = _rms_rows(acc[:, sl], gain[:, sl]).astype(out_ref.dtype)

        @pl.when((n >= nqk) & (n < nv))
        def _():
            out_ref[...] = acc.astype(out_ref.dtype)

        @pl.when(n >= nv)
        def _():
            out_ref[...] = _sigmoid(acc).astype(out_ref.dtype)
    elif epilogue == "ffn":
        ys = []
        for acc, acc_scr, cw_ref, cb_ref in zip(accs, acc_scrs, cw_refs, cb_refs):
            acc_scr[...] = acc
            cw = cw_ref[...]
            taps = cw.shape[0]
            y = cb_ref[...] + cw[taps - 1:taps, :] * acc[HALO:, :]
            for j in range(taps - 1):
                y = y + cw[j:j + 1, :] * acc_scr[pl.ds(HALO - (taps - 1) + j, tm), :]
            ys.append(y)
        u, gte = ys
        out_ref[...] = (gte * _sigmoid(gte) * u).astype(out_ref.dtype)
    elif epilogue == "lerp":
        acc = accs[0]
        acc_scrs[0][...] = acc
        cur = acc[HALO:, :]
        prev = acc_scrs[0][pl.ds(HALO - 1, tm), :]
        out_ref[...] = (cur + (prev - cur) * mu_ref[...]).astype(out_ref.dtype)
    else:
        raise ValueError(epilogue)


def _proj(x2d, gain_row, ws, *, seq, epilogue, out_dtype, extras=(), scale=1.0, tm=512, tn=512,
          w_col_offsets=None):
    M, K = x2d.shape
    n_w = len(ws) if w_col_offsets is None else len(w_col_offsets)
    halo = epilogue in ("ffn", "lerp")
    tm = min(tm, M, seq)
    assert M % tm == 0 and seq % tm == 0 and tm % HALO == 0
    if w_col_offsets is None:
        w_col_offsets = [0] * n_w
        n_out = ws[0].shape[1]
        w_list = list(ws)
    else:
        w_list = [ws[0]] * n_w
        n_out = ws[0].shape[1] // n_w
    assert n_out % tn == 0
    grid = (M // tm, n_out // tn)

    in_specs = [pl.BlockSpec((tm, K), lambda m, n: (m, 0))]
    args = [x2d]
    if halo:
        r = tm // HALO
        in_specs.append(pl.BlockSpec((HALO, K), lambda m, n: (jnp.maximum(m * r - 1, 0), 0)))
        args.append(x2d)
    in_specs.append(pl.BlockSpec((1, K), lambda m, n: (0, 0)))
    args.append(gain_row)
    for w, o in zip(w_list, w_col_offsets):
        in_specs.append(pl.BlockSpec((K, tn), functools.partial(lambda m, n, o: (0, n + o), o=o)))
        args.append(w)
    if epilogue == "ffn":
        cw, cb = extras
        taps = cw.shape[0]
        for o in w_col_offsets:
            in_specs.append(pl.BlockSpec((taps, tn), functools.partial(lambda m, n, o: (0, n + o), o=o)))
            args.append(cw)
        for o in w_col_offsets:
            in_specs.append(pl.BlockSpec((1, tn), functools.partial(lambda m, n, o: (0, n + o), o=o)))
            args.append(cb)
    elif epilogue in ("lerp", "fox"):
        in_specs.append(pl.BlockSpec((1, tn), lambda m, n: (0, n)))
        args.append(extras[0])

    scratch = [pltpu.VMEM((tm + (HALO if halo else 0), K), BF16)]
    if halo:
        scratch += [pltpu.VMEM((tm + HALO, tn), F32) for _ in range(n_w)]

    body = functools.partial(_proj_body, n_w=n_w, halo=halo, epilogue=epilogue, tm=tm, tn=tn,
                             tiles_per_seq=seq // tm, scale=scale, n_cols=n_out)
    return pl.pallas_call(
        body,
        grid=grid,
        in_specs=in_specs,
        out_specs=pl.BlockSpec((tm, tn), lambda m, n: (m, n)),
        out_shape=jax.ShapeDtypeStruct((M, n_out), out_dtype),
        scratch_shapes=scratch,
        compiler_params=_cparams("parallel", "arbitrary"),
        name="proj_" + epilogue,
    )(*args)


def _mm_res_body(a_ref, w_ref, r_ref, o_ref):
    o_ref[...] = r_ref[...] + _dot(a_ref[...], w_ref[...])


def _matmul_residual(a, w, res, *, tn=512):
    M, K = a.shape
    N = w.shape[1]
    tm = min(1024 if K <= 2048 else 512, M)
    assert M % tm == 0 and N % tn == 0
    return pl.pallas_call(
        _mm_res_body,
        grid=(M // tm, N // tn),
        in_specs=[pl.BlockSpec((tm, K), lambda m, n: (m, 0)),
                  pl.BlockSpec((K, tn), lambda m, n: (0, n)),
                  pl.BlockSpec((tm, tn), lambda m, n: (m, n))],
        out_specs=pl.BlockSpec((tm, tn), lambda m, n: (m, n)),
        out_shape=jax.ShapeDtypeStruct((M, N), F32),
        compiler_params=_cparams("parallel", "parallel"),
        name="matmul_residual",
    )(a, w, res)


def _forget_body(x_ref, g_ref, w_ref, b_ref, c_ref, carry):
    s = pl.program_id(1)

    @pl.when(s == 0)
    def _():
        carry[...] = jnp.zeros_like(carry)

    h = _rms_rows(x_ref[...], g_ref[...]).astype(BF16)
    z = _dot(h, w_ref[...]) + b_ref[...]
    logf = jnp.minimum(z, 0.0) - jnp.log(1.0 + jnp.exp(-jnp.abs(z)))
    tm = logf.shape[0]
    row = lax.broadcasted_iota(jnp.int32, (tm, tm), 0)
    col = lax.broadcasted_iota(jnp.int32, (tm, tm), 1)
    tri = (col <= row).astype(BF16)
    c = _dot_exact_lhs(tri, logf) + carry[...]
    c_ref[...] = c
    carry[...] = c[tm - 1:tm, :]


def _forget_cumsum(x, gain_row, w_f, b_f, *, tm=256):
    B, S, K = x.shape
    tm = min(tm, S)
    return pl.pallas_call(
        _forget_body,
        grid=(B, S // tm),
        in_specs=[pl.BlockSpec((None, tm, K), lambda b, s: (b, s, 0)),
                  pl.BlockSpec((1, K), lambda b, s: (0, 0)),
                  pl.BlockSpec((K, LANES), lambda b, s: (0, 0)),
                  pl.BlockSpec((1, LANES), lambda b, s: (0, 0))],
        out_specs=pl.BlockSpec((None, tm, LANES), lambda b, s: (b, s, 0)),
        out_shape=jax.ShapeDtypeStruct((B, S, LANES), F32),
        scratch_shapes=[pltpu.VMEM((1, LANES), F32)],
        compiler_params=_cparams("parallel", "arbitrary"),
        name="fox_forget_cumsum",
    )(x, gain_row, w_f, b_f)


def _fox_attn_body(q_ref, k_ref, v_ref, gate_ref, ck_ref, o_ref, m_scr, l_scr, acc_scr, *, tq, tk):
    i = pl.program_id(2)
    q = q_ref[...]
    m_scr[...] = jnp.full_like(m_scr, NEG_BIG)
    l_scr[...] = jnp.zeros_like(l_scr)
    acc_scr[...] = jnp.zeros_like(acc_scr)

    def block(j, masked):
        start = pl.multiple_of(j * tk, tk)
        k = k_ref[pl.ds(start, tk), :]
        v = v_ref[pl.ds(start, tk), :]
        s = _dot_nt(q, k) - ck_ref[pl.ds(j, 1), :]
        if masked:
            qpos = i * tq + lax.broadcasted_iota(jnp.int32, (tq, tk), 0)
            kpos = j * tk + lax.broadcasted_iota(jnp.int32, (tq, tk), 1)
            s = jnp.where(kpos <= qpos, s, NEG_BIG)
        m_old = m_scr[...]
        m_new = jnp.maximum(m_old, jnp.max(s, axis=-1, keepdims=True))
        alpha = jnp.exp(m_old - m_new)
        p = jnp.exp(s - m_new)
        l_scr[...] = alpha * l_scr[...] + jnp.sum(p, axis=-1, keepdims=True)
        acc_scr[...] = alpha * acc_scr[...] + _dot(p.astype(BF16), v)
        m_scr[...] = m_new

    n_full = (i * tq) // tk

    def full_step(j, c):
        block(j, False)
        return c

    lax.fori_loop(0, n_full, full_step, 0)
    block(n_full, True)
    o = acc_scr[...] / l_scr[...]
    o_ref[...] = (o * gate_ref[...].astype(F32)).astype(o_ref.dtype)


def _fox_attention(qkvg, ck_rows, *, tq=256, tk=512):
    B, S, D4 = qkvg.shape
    D = D4 // 4
    H = D // FOX_HEAD_DIM
    tk = min(tk, S)
    tq = min(tq, tk)
    assert tk % tq == 0 and S % tk == 0
    body = functools.partial(_fox_attn_body, tq=tq, tk=tk)
    return pl.pallas_call(
        body,
        grid=(B, H, S // tq),
        in_specs=[pl.BlockSpec((None, tq, FOX_HEAD_DIM), lambda b, h, i: (b, i, h)),
                  pl.BlockSpec((None, S, FOX_HEAD_DIM), lambda b, h, i: (b, 0, H + h)),
                  pl.BlockSpec((None, S, FOX_HEAD_DIM), lambda b, h, i: (b, 0, 2 * H + h)),
                  pl.BlockSpec((None, tq, FOX_HEAD_DIM), lambda b, h, i: (b, i, 3 * H + h)),
                  pl.BlockSpec((None, None, S // tk, tk), lambda b, h, i: (b, h, 0, 0))],
        out_specs=pl.BlockSpec((None, tq, FOX_HEAD_DIM), lambda b, h, i: (b, i, h)),
        out_shape=jax.ShapeDtypeStruct((B, S, D), BF16),
        scratch_shapes=[pltpu.VMEM((tq, 1), F32), pltpu.VMEM((tq, 1), F32),
                        pltpu.VMEM((tq, FOX_HEAD_DIM), F32)],
        compiler_params=_cparams("parallel", "parallel", "arbitrary"),
        name="fox_attention",
    )(qkvg, qkvg, qkvg, qkvg, ck_rows)


def _cross_attn_body(q_ref, k_ref, v_ref, o_ref, *, heads):
    dh = q_ref.shape[-1] // heads
    for h in range(heads):
        sl = slice(h * dh, (h + 1) * dh)
        s = _dot_nt(q_ref[:, sl], k_ref[:, sl])
        s = s - jnp.max(s, axis=-1, keepdims=True)
        p = jnp.exp(s)
        l = jnp.sum(p, axis=-1, keepdims=True)
        o = _dot(p.astype(BF16), v_ref[:, sl])
        o_ref[:, sl] = (o / l).astype(o_ref.dtype)


def _cross_attention(q, kv_all, layer, *, tq=512):
    B, S, D = q.shape
    Mem = kv_all.shape[1]
    tq = min(tq, S)
    body = functools.partial(_cross_attn_body, heads=MEM_HEADS)
    return pl.pallas_call(
        body,
        grid=(B, S // tq),
        in_specs=[pl.BlockSpec((None, tq, D), lambda b, i: (b, i, 0)),
                  pl.BlockSpec((None, Mem, D), lambda b, i: (b, 0, 2 * layer)),
                  pl.BlockSpec((None, Mem, D), lambda b, i: (b, 0, 2 * layer + 1))],
        out_specs=pl.BlockSpec((None, tq, D), lambda b, i: (b, i, 0)),
        out_shape=jax.ShapeDtypeStruct((B, S, D), BF16),
        compiler_params=_cparams("parallel", "parallel"),
        name="cross_attention",
    )(q, kv_all, kv_all)


def _conformer_body(z_ref, zp_ref, w_ref, b_ref, g_ref, beta_ref, o_ref, zs, ys, *, tm, width):
    i = pl.program_id(1)
    zs[pl.ds(CONV_HALO, tm), :] = z_ref[...]
    zs[pl.ds(0, CONV_HALO), :] = jnp.where(i == 0, 0.0, zp_ref[...])
    D = z_ref.shape[-1]
    base = CONV_HALO - (width - 1)
    for c in range(D // LANES):
        sl = slice(c * LANES, (c + 1) * LANES)
        acc = jnp.broadcast_to(b_ref[:, sl], (tm, LANES))
        for k in range(width):
            acc = acc + w_ref[k:k + 1, sl] * zs[pl.ds(base + k, tm), sl]
        ys[:, sl] = acc
    y = ys[...]
    mu = jnp.mean(y, axis=-1, keepdims=True)
    d = y - mu
    var = jnp.mean(d * d, axis=-1, keepdims=True)
    t = d * lax.rsqrt(var + LN_EPS) * g_ref[...] + beta_ref[...]
    o_ref[...] = (t * _sigmoid(t)).astype(o_ref.dtype)


def _conformer_conv(z, dw_w, dw_b, ln_g, ln_b, *, tm=256):
    B, S, D = z.shape
    width = dw_w.shape[0]
    tm = min(tm, S)
    assert width - 1 <= CONV_HALO and tm % CONV_HALO == 0 and S % tm == 0
    r = tm // CONV_HALO
    body = functools.partial(_conformer_body, tm=tm, width=width)
    row = lambda a: a.reshape(1, D)
    return pl.pallas_call(
        body,
        grid=(B, S // tm),
        in_specs=[pl.BlockSpec((None, tm, D), lambda b, i: (b, i, 0)),
                  pl.BlockSpec((None, CONV_HALO, D), lambda b, i: (b, jnp.maximum(i * r - 1, 0), 0)),
                  pl.BlockSpec((width, D), lambda b, i: (0, 0)),
                  pl.BlockSpec((1, D), lambda b, i: (0, 0)),
                  pl.BlockSpec((1, D), lambda b, i: (0, 0)),
                  pl.BlockSpec((1, D), lambda b, i: (0, 0))],
        out_specs=pl.BlockSpec((None, tm, D), lambda b, i: (b, i, 0)),
        out_shape=jax.ShapeDtypeStruct((B, S, D), BF16),
        scratch_shapes=[pltpu.VMEM((tm + CONV_HALO, D), F32), pltpu.VMEM((tm, D), F32)],
        compiler_params=_cparams("parallel", "parallel"),
        name="conformer_conv_ln",
    )(z, z, dw_w, row(dw_b), row(ln_g), row(ln_b))


def _head_sum(x, head_dim):
    n = x.shape[-1]
    r = lax.broadcasted_iota(jnp.int32, (LANES, LANES), 0) // head_dim
    c = lax.broadcasted_iota(jnp.int32, (LANES, LANES), 1) // head_dim
    ones = (r == c).astype(BF16)
    parts = []
    for j in range(n // LANES):
        parts.append(_dot_exact_rhs(x[:, j * LANES:(j + 1) * LANES], ones))
    return parts[0] if len(parts) == 1 else jnp.concatenate(parts, axis=-1)


def _rwkv_prep_body(p_ref, w0_ref, db_ref, a0_ref, ib_ref, gb_ref, kk_ref, ka_ref, rk_ref,
                    r_o, k_o, v_o, kk_o, b_o, lw_o, bonus_o, g_o, *, D):
    lw_pad = LANES
    r = p_ref[:, 0:D]
    k = p_ref[:, D:2 * D]
    v = p_ref[:, 2 * D:3 * D]
    o = 3 * D
    w_lo = p_ref[:, o:o + lw_pad]
    a_lo = p_ref[:, o + lw_pad:o + 2 * lw_pad]
    g_lo = p_ref[:, o + 2 * lw_pad:o + 2 * lw_pad + GATE_LORA]

    w_arg = w0_ref[...] + _dot(jnp.tanh(w_lo).astype(BF16), db_ref[...])
    softplus = jnp.maximum(-w_arg, 0.0) + jnp.log(1.0 + jnp.exp(-jnp.abs(w_arg)))
    lw_o[...] = -jnp.exp(-softplus - 0.5)
    a = _sigmoid(a0_ref[...] + _dot(a_lo.astype(BF16), ib_ref[...]))
    g_o[...] = _dot(_sigmoid(g_lo).astype(BF16), gb_ref[...])
    kk = k * kk_ref[...]
    nrm = jnp.sqrt(_head_sum(kk * kk, RWKV_HEAD_DIM))
    kk = kk / jnp.maximum(nrm, 1e-12)
    k2 = k * (1.0 + (a - 1.0) * ka_ref[...])
    r_o[...] = r
    k_o[...] = k2
    v_o[...] = v
    kk_o[...] = kk
    b_o[...] = kk * a
    bonus_o[...] = _head_sum(r * k2 * rk_ref[...], RWKV_HEAD_DIM) * v


def _rwkv_prep(proj, w0, decay_b, a0, iclr_b, gate_b, k_k, k_a, r_k, *, D, tm=256):
    M, P = proj.shape
    tm = min(tm, M)
    body = functools.partial(_rwkv_prep_body, D=D)
    row = lambda a: a.reshape(1, D)
    full = lambda a: pl.BlockSpec(a.shape, lambda m: (0, 0))
    ins = [proj, row(w0), decay_b, row(a0), iclr_b, gate_b, row(k_k), row(k_a), row(r_k)]
    out_spec = pl.BlockSpec((tm, D), lambda m: (m, 0))
    return pl.pallas_call(
        body,
        grid=(M // tm,),
        in_specs=[pl.BlockSpec((tm, P), lambda m: (m, 0))] + [full(a) for a in ins[1:]],
        out_specs=[out_spec] * 8,
        out_shape=[jax.ShapeDtypeStruct((M, D), F32)] * 8,
        compiler_params=_cparams("parallel"),
        name="rwkv_prep",
    )(*ins)


def _mm3(a, b, dot):
    ah = a.astype(BF16)
    al = (a - ah.astype(F32)).astype(BF16)
    bh = b.astype(BF16)
    bl = (b - bh.astype(F32)).astype(BF16)
    return dot(ah, bh) + (dot(ah, bl) + dot(al, bh))


def _rwkv_scan_body(r_ref, k_ref, v_ref, kk_ref, b_ref, lw_ref, bonus_ref, g_ref, lnw_ref, lnb_ref,
                    o_ref, state, *, C, D):
    c_idx = pl.program_id(1)

    @pl.when(c_idx == 0)
    def _():
        state[...] = jnp.zeros_like(state)

    N = RWKV_HEAD_DIM
    row = lax.broadcasted_iota(jnp.int32, (C, C), 0)
    col = lax.broadcasted_iota(jnp.int32, (C, C), 1)
    tri_incl = (col <= row)
    tri_strict = (col < row)
    eye = (col == row).astype(F32)

    lw = lw_ref[...]
    cum = _dot_exact_lhs(tri_incl.astype(BF16), lw)
    cum_last = cum[C - 1:C, :]
    g_incl = jnp.exp(cum)
    g_inv = jnp.exp(-cum)
    g_tail = jnp.exp(cum_last - cum)
    r_t = r_ref[...] * g_incl
    kk_t = kk_ref[...] * jnp.exp(cum - lw)
    k_t = k_ref[...] * g_inv
    b_t = b_ref[...] * g_inv
    k_h = k_ref[...] * g_tail
    b_h = b_ref[...] * g_tail
    g_last = jnp.exp(cum_last)
    v_all = v_ref[...]

    ys = []
    for h in range(D // N):
        sl = slice(h * N, (h + 1) * N)
        kkh, rh, kh, bh, vh = kk_t[:, sl], r_t[:, sl], k_t[:, sl], b_t[:, sl], v_all[:, sl]
        L = jnp.where(tri_strict, _mm3(kkh, bh, _dot_nt), 0.0)
        Akk = jnp.where(tri_strict, _mm3(kkh, kh, _dot_nt), 0.0)
        Arb = jnp.where(tri_incl, _mm3(rh, bh, _dot_nt), 0.0)
        Ark = jnp.where(tri_incl, _mm3(rh, kh, _dot_nt), 0.0)
        T = eye - L
        P = L
        span = 2
        while span < C:
            P = _mm3(P, P, _dot)
            T = T + _mm3(T, P, _dot)
            span *= 2
        S0 = state[h]
        rhs = _mm3(kkh, S0, _dot_nt) + _mm3(Akk, vh, _dot)
        E = -_mm3(T, rhs, _dot)
        y = _mm3(rh, S0, _dot_nt) + _mm3(Arb, E, _dot) + _mm3(Ark, vh, _dot)
        state[h] = S0 * g_last[:, sl] + _mm3(E, b_h[:, sl], _dot_tn) + _mm3(vh, k_h[:, sl], _dot_tn)
        ys.append(y)
    y = jnp.concatenate(ys, axis=-1)

    mean = _head_sum(y, N) * (1.0 / N)
    d = y - mean
    var = _head_sum(d * d, N) * (1.0 / N)
    yn = d * lax.rsqrt(var + RWKV_GN_EPS) * lnw_ref[...] + lnb_ref[...] + bonus_ref[...]
    o_ref[...] = (yn * g_ref[...]).astype(o_ref.dtype)


def _rwkv_scan(r, k, v, kk, b, lw, bonus, g, ln_w, ln_b, *, B, S):
    M, D = r.shape
    C = min(RWKV_CHUNK, S)
    nc = S // C
    body = functools.partial(_rwkv_scan_body, C=C, D=D)
    tile = pl.BlockSpec((C, D), lambda bi, c: (bi * nc + c, 0))
    vec = pl.BlockSpec((1, D), lambda bi, c: (0, 0))
    return pl.pallas_call(
        body,
        grid=(B, nc),
        in_specs=[tile] * 8 + [vec, vec],
        out_specs=tile,
        out_shape=jax.ShapeDtypeStruct((M, D), BF16),
        scratch_shapes=[pltpu.VMEM((D // RWKV_HEAD_DIM, RWKV_HEAD_DIM, RWKV_HEAD_DIM), F32)],
        compiler_params=_cparams("parallel", "arbitrary"),
        name="rwkv_scan",
    )(r, k, v, kk, b, lw, bonus, g, ln_w.reshape(1, D), ln_b.reshape(1, D))


def _rmsnorm_body(x_ref, g_ref, o_ref):
    o_ref[...] = _rms_rows(x_ref[...], g_ref[...])


def _rmsnorm(x2d, gain_row, *, tm=512):
    M, K = x2d.shape
    tm = min(tm, M)
    return pl.pallas_call(
        _rmsnorm_body,
        grid=(M // tm,),
        in_specs=[pl.BlockSpec((tm, K), lambda m: (m, 0)), pl.BlockSpec((1, K), lambda m: (0, 0))],
        out_specs=pl.BlockSpec((tm, K), lambda m: (m, 0)),
        out_shape=jax.ShapeDtypeStruct((M, K), F32),
        compiler_params=_cparams("parallel"),
        name="final_rmsnorm",
    )(x2d, gain_row)


def _pad_cols(a, n):
    return jnp.pad(a, ((0, 0), (0, n - a.shape[1])))


def _pad_rows(a, n):
    return jnp.pad(a, ((0, n - a.shape[0]), (0, 0)))


def _fox_layer(x2d, B, S, gain_row, w_in, b_f, q_gain, k_gain, w_o):
    M, D = x2d.shape
    H = FOX_HEADS
    w_main = w_in[:, :4 * D].astype(BF16)
    w_f = _pad_cols(w_in[:, 4 * D:], LANES).astype(BF16)
    b_row = _pad_cols(b_f.reshape(1, H), LANES)
    scale = FOX_HEAD_DIM ** -0.5
    gains = jnp.concatenate([jnp.tile(q_gain * scale, H), jnp.tile(k_gain, H),
                             jnp.ones((2 * D,), F32)]).reshape(1, 4 * D)
    qkvg = _proj(x2d, gain_row, [w_main], seq=S, epilogue="fox", out_dtype=BF16, extras=(gains,))
    c = _forget_cumsum(x2d.reshape(B, S, D), gain_row, w_f, b_row)
    tk = min(512, S)
    ck_rows = c[:, :, :H].transpose(0, 2, 1).reshape(B, H, S // tk, tk)
    o = _fox_attention(qkvg.reshape(B, S, 4 * D), ck_rows, tk=tk)
    return _matmul_residual(o.reshape(M, D), w_o.astype(BF16), x2d)


def _rwkv_layer(x2d, B, S, gain_row, w_in, mu, w0, decay_b, a0, iclr_b, gate_b, k_k, k_a, r_k,
                ln_w, ln_b, w_o):
    M, D = x2d.shape
    o3 = 3 * D
    segs = [(0, o3, o3), (o3, DECAY_LORA, LANES), (o3 + DECAY_LORA, ICLR_LORA, LANES),
            (o3 + DECAY_LORA + ICLR_LORA, GATE_LORA, GATE_LORA)]
    w_pad = jnp.concatenate([_pad_cols(w_in[:, s:s + n], p) for s, n, p in segs], axis=1)
    mu_pad = jnp.concatenate([_pad_cols(mu[None, s:s + n], p) for s, n, p in segs], axis=1)
    P = w_pad.shape[1]
    tn = 512
    Pp = -(-P // tn) * tn
    w_pad = _pad_cols(w_pad, Pp).astype(BF16)
    mu_pad = _pad_cols(mu_pad, Pp)
    proj = _proj(x2d, gain_row, [w_pad], seq=S, epilogue="lerp", out_dtype=F32, extras=(mu_pad,), tn=tn)
    r, k, v, kk, b, lw, bonus, g = _rwkv_prep(
        proj, w0, _pad_rows(decay_b, LANES).astype(BF16), a0, _pad_rows(iclr_b, LANES).astype(BF16),
        gate_b.astype(BF16), k_k, k_a, r_k.reshape(-1), D=D)
    y = _rwkv_scan(r, k, v, kk, b, lw, bonus, g, ln_w, ln_b, B=B, S=S)
    return _matmul_residual(y, w_o.astype(BF16), x2d)


def _conformer_layer(x2d, B, S, gain_row, w_in, dw_w, dw_b, ln_g, ln_b, w_o):
    M, D = x2d.shape
    tn = 512
    z = _proj(x2d, gain_row, [w_in.astype(BF16)], seq=S, epilogue="glu", out_dtype=F32,
              w_col_offsets=[0, D // tn], tn=tn)
    t = _conformer_conv(z.reshape(B, S, D), dw_w, dw_b, ln_g, ln_b)
    return _matmul_residual(t.reshape(M, D), w_o.astype(BF16), x2d)


def _cross_layer(x2d, B, S, gain_row, kv_all, layer, w_q, w_o):
    M, D = x2d.shape
    scale = (D // MEM_HEADS) ** -0.5
    q = _proj(x2d, gain_row, [w_q.astype(BF16)], seq=S, epilogue="scale_bf16", out_dtype=BF16, scale=scale)
    o = _cross_attention(q.reshape(B, S, D), kv_all, layer)
    return _matmul_residual(o.reshape(M, D), w_o.astype(BF16), x2d)


def _ffn_layer(x2d, S, gain_row, w_up, dw_w, dw_b, w_down):
    d_ff = w_down.shape[0]
    tn = 512
    act = _proj(x2d, gain_row, [w_up.astype(BF16)], seq=S, epilogue="ffn", out_dtype=BF16,
                extras=(dw_w, dw_b.reshape(1, -1)), w_col_offsets=[0, d_ff // tn], tn=tn)
    return _matmul_residual(act, w_down.astype(BF16), x2d)


def kernel(x, mem, mem_norm, norm_mix, norm_cross, norm_ffn, final_norm, fox_w_in, fox_b_f, fox_q_gain, fox_k_gain, fox_w_o, rwkv_w_in, rwkv_mu, rwkv_w0, rwkv_decay_b, rwkv_a0, rwkv_iclr_b, rwkv_gate_b, rwkv_k_k, rwkv_k_a, rwkv_r_k, rwkv_ln_w, rwkv_ln_b, rwkv_w_o, conv_w_in, conv_dw_w, conv_dw_b, conv_ln_g, conv_ln_b, conv_w_o, cross_w_q, cross_w_kv, cross_w_o, ffn_w_up, ffn_dw_w, ffn_dw_b, ffn_w_down):
    B, S, D = x.shape
    depth = norm_mix.shape[0]
    Mem = mem.shape[1]
    M = B * S
    x2d = x.reshape(M, D)

    w_kv_all = jnp.concatenate([cross_w_kv[i] for i in range(depth)], axis=1).astype(BF16)
    kv_all = _proj(mem.reshape(B * Mem, D), mem_norm.reshape(1, D), [w_kv_all], seq=Mem,
                   epilogue="scale_bf16", out_dtype=BF16).reshape(B, Mem, depth * 2 * D)

    for i in range(depth):
        kind, j = i % 3, i // 3
        g_mix = norm_mix[i].reshape(1, D)
        if kind == 0:
            x2d = _fox_layer(x2d, B, S, g_mix, fox_w_in[j], fox_b_f[j], fox_q_gain[j], fox_k_gain[j],
                             fox_w_o[j])
        elif kind == 1:
            x2d = _rwkv_layer(x2d, B, S, g_mix, rwkv_w_in[j], rwkv_mu[j], rwkv_w0[j], rwkv_decay_b[j],
                              rwkv_a0[j], rwkv_iclr_b[j], rwkv_gate_b[j], rwkv_k_k[j], rwkv_k_a[j],
                              rwkv_r_k[j], rwkv_ln_w[j], rwkv_ln_b[j], rwkv_w_o[j])
        else:
            x2d = _conformer_layer(x2d, B, S, g_mix, conv_w_in[j], conv_dw_w[j], conv_dw_b[j],
                                   conv_ln_g[j], conv_ln_b[j], conv_w_o[j])
        x2d = _cross_layer(x2d, B, S, norm_cross[i].reshape(1, D), kv_all, i, cross_w_q[i], cross_w_o[i])
        x2d = _ffn_layer(x2d, S, norm_ffn[i].reshape(1, D), ffn_w_up[i], ffn_dw_w[i], ffn_dw_b[i],
                         ffn_w_down[i])
    return _rmsnorm(x2d, final_norm.reshape(1, D)).reshape(B, S, D)
```

```python
import functools

import jax
import jax.numpy as jnp
from jax import lax
from jax.experimental import pallas as pl
from jax.experimental.pallas import tpu as pltpu

F32 = jnp.float32
BF16 = jnp.bfloat16

RMS_EPS = 1e-6
LN_EPS = 1e-5
RWKV_GN_EPS = 64e-5

LANES = 128
HALO = 16
VMEM_LIMIT = 56 * 1024 * 1024

FOX_HEADS = 16
FOX_HEAD_DIM = 128
MEM_HEADS = 4
RWKV_HEAD_DIM = 64
RWKV_CHUNK = 64
DECAY_LORA = 96
ICLR_LORA = 96
GATE_LORA = 256
CONV_HALO = 32
NEG_BIG = -1e30
LOG2E = 1.4426950408889634


def _cparams(*sem):
    return pltpu.CompilerParams(dimension_semantics=sem, vmem_limit_bytes=VMEM_LIMIT)


def _rms_rows(x, g):
    ms = jnp.mean(x * x, axis=-1, keepdims=True)
    return x * lax.rsqrt(ms + RMS_EPS) * g


def _sigmoid(x):
    return 1.0 / (1.0 + jnp.exp(-x))


def _dot(a, b):
    return jnp.dot(a, b, preferred_element_type=F32)


def _dot_nt(a, b):
    return lax.dot_general(a, b, (((1,), (1,)), ((), ())), preferred_element_type=F32)


def _dot_tn(a, b):
    return lax.dot_general(a, b, (((0,), (0,)), ((), ())), preferred_element_type=F32)


def _split3(x):
    hi = x.astype(BF16)
    r1 = x - hi.astype(F32)
    mid = r1.astype(BF16)
    lo = (r1 - mid.astype(F32)).astype(BF16)
    return hi, mid, lo


def _dot_exact_lhs(sel, x):
    hi, mid, lo = _split3(x)
    return _dot(sel, hi) + _dot(sel, mid) + _dot(sel, lo)


def _dot_exact_rhs(x, sel):
    hi, mid, lo = _split3(x)
    return _dot(hi, sel) + _dot(mid, sel) + _dot(lo, sel)


def _proj_body(*refs, n_w, halo, epilogue, tm, tn, tiles_per_seq, scale, n_cols):
    it = iter(refs)
    x_ref = next(it)
    xp_ref = next(it) if halo else None
    g_ref = next(it)
    w_refs = [next(it) for _ in range(n_w)]
    if epilogue == "ffn":
        cw_refs = [next(it) for _ in range(n_w)]
        cb_refs = [next(it) for _ in range(n_w)]
    elif epilogue == "lerp":
        mu_ref = next(it)
    elif epilogue == "fox":
        gain_ref = next(it)
    out_ref = next(it)
    h_scr = next(it)
    acc_scrs = [next(it) for _ in range(n_w)] if halo else None

    m = pl.program_id(0)
    n = pl.program_id(1)
    off = HALO if halo else 0

    @pl.when(n == 0)
    def _():
        g = g_ref[...]
        h_scr[pl.ds(off, tm), :] = _rms_rows(x_ref[...], g).astype(BF16)
        if halo:
            hp = _rms_rows(xp_ref[...], g)
            first = (m % tiles_per_seq) == 0
            h_scr[pl.ds(0, HALO), :] = jnp.where(first, 0.0, hp).astype(BF16)

    h = h_scr[...]
    accs = [_dot(h, w_ref[...]) for w_ref in w_refs]

    if epilogue == "scale_bf16":
        out_ref[...] = (accs[0] * scale).astype(out_ref.dtype)
    elif epilogue == "glu":
        out_ref[...] = (accs[0] * _sigmoid(accs[1])).astype(out_ref.dtype)
    elif epilogue == "fox":
        nqk = 2 * n_cols // 4 // tn
        nv = 3 * n_cols // 4 // tn
        acc = accs[0]

        @pl.when(n < nqk)
        def _():
            gain = gain_ref[...]
            for j in range(tn // FOX_HEAD_DIM):
                sl = slice(j * FOX_HEAD_DIM, (j + 1) * FOX_HEAD_DIM)
                out_ref[:, sl] = _rms_rows(acc[:, sl], gain[:, sl]).astype(out_ref.dtype)

        @pl.when((n >= nqk) & (n < nv))
        def _():
            out_ref[...] = acc.astype(out_ref.dtype)

        @pl.when(n >= nv)
        def _():
            out_ref[...] = _sigmoid(acc).astype(out_ref.dtype)
    elif epilogue == "ffn":
        ys = []
        for acc, acc_scr, cw_ref, cb_ref in zip(accs, acc_scrs, cw_refs, cb_refs):
            acc_scr[...] = acc
            cw = cw_ref[...]
            taps = cw.shape[0]
            y = cb_ref[...] + cw[taps - 1:taps, :] * acc[HALO:, :]
            for j in range(taps - 1):
                y = y + cw[j:j + 1, :] * acc_scr[pl.ds(HALO - (taps - 1) + j, tm), :]
            ys.append(y)
        u, gte = ys
        out_ref[...] = (gte * _sigmoid(gte) * u).astype(out_ref.dtype)
    elif epilogue == "lerp":
        acc = accs[0]
        acc_scrs[0][...] = acc
        cur = acc[HALO:, :]
        prev = acc_scrs[0][pl.ds(HALO - 1, tm), :]
        out_ref[...] = (cur + (prev - cur) * mu_ref[...]).astype(out_ref.dtype)
    else:
        raise ValueError(epilogue)


def _proj(x2d, gain_row, ws, *, seq, epilogue, out_dtype, extras=(), scale=1.0, tm=512, tn=512,
          w_col_offsets=None):
    M, K = x2d.shape
    n_w = len(ws) if w_col_offsets is None else len(w_col_offsets)
    halo = epilogue in ("ffn", "lerp")
    tm = min(tm, M, seq)
    assert M % tm == 0 and seq % tm == 0 and tm % HALO == 0
    if w_col_offsets is None:
        w_col_offsets = [0] * n_w
        n_out = ws[0].shape[1]
        w_list = list(ws)
    else:
        w_list = [ws[0]] * n_w
        n_out = ws[0].shape[1] // n_w
    assert n_out % tn == 0
    grid = (M // tm, n_out // tn)

    in_specs = [pl.BlockSpec((tm, K), lambda m, n: (m, 0))]
    args = [x2d]
    if halo:
        r = tm // HALO
        in_specs.append(pl.BlockSpec((HALO, K), lambda m, n: (jnp.maximum(m * r - 1, 0), 0)))
        args.append(x2d)
    in_specs.append(pl.BlockSpec((1, K), lambda m, n: (0, 0)))
    args.append(gain_row)
    for w, o in zip(w_list, w_col_offsets):
        in_specs.append(pl.BlockSpec((K, tn), functools.partial(lambda m, n, o: (0, n + o), o=o)))
        args.append(w)
    if epilogue == "ffn":
        cw, cb = extras
        taps = cw.shape[0]
        for o in w_col_offsets:
            in_specs.append(pl.BlockSpec((taps, tn), functools.partial(lambda m, n, o: (0, n + o), o=o)))
            args.append(cw)
        for o in w_col_offsets:
            in_specs.append(pl.BlockSpec((1, tn), functools.partial(lambda m, n, o: (0, n + o), o=o)))
            args.append(cb)
    elif epilogue in ("lerp", "fox"):
        in_specs.append(pl.BlockSpec((1, tn), lambda m, n: (0, n)))
        args.append(extras[0])

    scratch = [pltpu.VMEM((tm + (HALO if halo else 0), K), BF16)]
    if halo:
        scratch += [pltpu.VMEM((tm + HALO, tn), F32) for _ in range(n_w)]

    body = functools.partial(_proj_body, n_w=n_w, halo=halo, epilogue=epilogue, tm=tm, tn=tn,
                             tiles_per_seq=seq // tm, scale=scale, n_cols=n_out)
    return pl.pallas_call(
        body,
        grid=grid,
        in_specs=in_specs,
        out_specs=pl.BlockSpec((tm, tn), lambda m, n: (m, n)),
        out_shape=jax.ShapeDtypeStruct((M, n_out), out_dtype),
        scratch_shapes=scratch,
        compiler_params=_cparams("parallel", "arbitrary"),
        name="proj_" + epilogue,
    )(*args)


def _mm_res_body(a_ref, w_ref, r_ref, o_ref):
    o_ref[...] = r_ref[...] + _dot(a_ref[...], w_ref[...])


def _matmul_residual(a, w, res, *, tn=512):
    M, K = a.shape
    N = w.shape[1]
    tm = min(1024 if K <= 2048 else 512, M)
    assert M % tm == 0 and N % tn == 0
    return pl.pallas_call(
        _mm_res_body,
        grid=(M // tm, N // tn),
        in_specs=[pl.BlockSpec((tm, K), lambda m, n: (m, 0)),
                  pl.BlockSpec((K, tn), lambda m, n: (0, n)),
                  pl.BlockSpec((tm, tn), lambda m, n: (m, n))],
        out_specs=pl.BlockSpec((tm, tn), lambda m, n: (m, n)),
        out_shape=jax.ShapeDtypeStruct((M, N), F32),
        compiler_params=_cparams("parallel", "parallel"),
        name="matmul_residual",
    )(a, w, res)


def _forget_body(x_ref, g_ref, w_ref, b_ref, c_ref, carry):
    s = pl.program_id(1)

    @pl.when(s == 0)
    def _():
        carry[...] = jnp.zeros_like(carry)

    h = _rms_rows(x_ref[...], g_ref[...]).astype(BF16)
    z = _dot(h, w_ref[...]) + b_ref[...]
    logf = jnp.minimum(z, 0.0) - jnp.log(1.0 + jnp.exp(-jnp.abs(z)))
    tm = logf.shape[0]
    row = lax.broadcasted_iota(jnp.int32, (tm, tm), 0)
    col = lax.broadcasted_iota(jnp.int32, (tm, tm), 1)
    tri = (col <= row).astype(BF16)
    c = _dot_exact_lhs(tri, logf) + carry[...]
    c_ref[...] = c * LOG2E
    carry[...] = c[tm - 1:tm, :]


def _forget_cumsum(x, gain_row, w_f, b_f, *, tm=256):
    B, S, K = x.shape
    tm = min(tm, S)
    return pl.pallas_call(
        _forget_body,
        grid=(B, S // tm),
        in_specs=[pl.BlockSpec((None, tm, K), lambda b, s: (b, s, 0)),
                  pl.BlockSpec((1, K), lambda b, s: (0, 0)),
                  pl.BlockSpec((K, LANES), lambda b, s: (0, 0)),
                  pl.BlockSpec((1, LANES), lambda b, s: (0, 0))],
        out_specs=pl.BlockSpec((None, tm, LANES), lambda b, s: (b, s, 0)),
        out_shape=jax.ShapeDtypeStruct((B, S, LANES), F32),
        scratch_shapes=[pltpu.VMEM((1, LANES), F32)],
        compiler_params=_cparams("parallel", "arbitrary"),
        name="fox_forget_cumsum",
    )(x, gain_row, w_f, b_f)


def _fox_attn_body(q_ref, k_ref, v_ref, gate_ref, ck_ref, o_ref, m_scr, l_scr, acc_scr, *, tq, tk, heads):
    i = pl.program_id(2)
    hd = FOX_HEAD_DIM
    rep = tk // LANES
    m_scr[...] = jnp.full_like(m_scr, NEG_BIG)
    l_scr[...] = jnp.zeros_like(l_scr)
    acc_scr[...] = jnp.zeros_like(acc_scr)

    def block(j, masked):
        start = pl.multiple_of(j * tk, tk)
        if masked:
            qpos = i * tq + lax.broadcasted_iota(jnp.int32, (tq, tk), 0)
            kpos = j * tk + lax.broadcasted_iota(jnp.int32, (tq, tk), 1)
            visible = kpos <= qpos
        for h in range(heads):
            sl = slice(h * hd, (h + 1) * hd)
            s = _dot_nt(q_ref[:, sl], k_ref[pl.ds(start, tk), sl]) - ck_ref[h, pl.ds(j, 1), :]
            if masked:
                s = jnp.where(visible, s, NEG_BIG)
            m_old = m_scr[h]
            m_new = jnp.maximum(m_old, jnp.max(s, axis=-1, keepdims=True))
            alpha = jnp.exp2(m_old - m_new)
            p = jnp.exp2(s - jnp.tile(m_new, (1, rep)))
            l_scr[h] = alpha * l_scr[h] + jnp.sum(p, axis=-1, keepdims=True)
            acc_scr[h] = alpha * acc_scr[h] + _dot(p.astype(BF16), v_ref[pl.ds(start, tk), sl])
            m_scr[h] = m_new

    n_full = (i * tq) // tk

    def pair_step(jj, c):
        block(2 * jj, False)
        block(2 * jj + 1, False)
        return c

    lax.fori_loop(0, n_full // 2, pair_step, 0)

    @pl.when(n_full % 2 == 1)
    def _():
        block(n_full - 1, False)

    block(n_full, True)
    for h in range(heads):
        sl = slice(h * hd, (h + 1) * hd)
        o = acc_scr[h] / l_scr[h]
        o_ref[:, sl] = (o * gate_ref[:, sl].astype(F32)).astype(o_ref.dtype)


def _fox_attention(qkvg, ck_rows, *, tq=512, tk=512, heads=2):
    B, S, D4 = qkvg.shape
    D = D4 // 4
    H = D // FOX_HEAD_DIM
    HG = H // heads
    W = heads * FOX_HEAD_DIM
    tk = min(tk, S)
    tq = min(tq, tk)
    assert tk % tq == 0 and S % tk == 0 and H % heads == 0
    body = functools.partial(_fox_attn_body, tq=tq, tk=tk, heads=heads)
    return pl.pallas_call(
        body,
        grid=(B, HG, S // tq),
        in_specs=[pl.BlockSpec((None, tq, W), lambda b, h, i: (b, i, h)),
                  pl.BlockSpec((None, S, W), lambda b, h, i: (b, 0, HG + h)),
                  pl.BlockSpec((None, S, W), lambda b, h, i: (b, 0, 2 * HG + h)),
                  pl.BlockSpec((None, tq, W), lambda b, h, i: (b, i, 3 * HG + h)),
                  pl.BlockSpec((None, heads, S // tk, tk), lambda b, h, i: (b, h, 0, 0))],
        out_specs=pl.BlockSpec((None, tq, W), lambda b, h, i: (b, i, h)),
        out_shape=jax.ShapeDtypeStruct((B, S, D), BF16),
        scratch_shapes=[pltpu.VMEM((heads, tq, LANES), F32), pltpu.VMEM((heads, tq, LANES), F32),
                        pltpu.VMEM((heads, tq, FOX_HEAD_DIM), F32)],
        compiler_params=_cparams("parallel", "parallel", "arbitrary"),
        name="fox_attention",
    )(qkvg, qkvg, qkvg, qkvg, ck_rows)


def _cross_attn_body(q_ref, k_ref, v_ref, o_ref, *, heads):
    dh = q_ref.shape[-1] // heads
    for h in range(heads):
        sl = slice(h * dh, (h + 1) * dh)
        s = _dot_nt(q_ref[:, sl], k_ref[:, sl])
        s = s - jnp.max(s, axis=-1, keepdims=True)
        p = jnp.exp(s)
        l = jnp.sum(p, axis=-1, keepdims=True)
        o = _dot(p.astype(BF16), v_ref[:, sl])
        o_ref[:, sl] = (o / l).astype(o_ref.dtype)


def _cross_attention(q, kv_all, layer, *, tq=512):
    B, S, D = q.shape
    Mem = kv_all.shape[1]
    tq = min(tq, S)
    body = functools.partial(_cross_attn_body, heads=MEM_HEADS)
    return pl.pallas_call(
        body,
        grid=(B, S // tq),
        in_specs=[pl.BlockSpec((None, tq, D), lambda b, i: (b, i, 0)),
                  pl.BlockSpec((None, Mem, D), lambda b, i: (b, 0, 2 * layer)),
                  pl.BlockSpec((None, Mem, D), lambda b, i: (b, 0, 2 * layer + 1))],
        out_specs=pl.BlockSpec((None, tq, D), lambda b, i: (b, i, 0)),
        out_shape=jax.ShapeDtypeStruct((B, S, D), BF16),
        compiler_params=_cparams("parallel", "parallel"),
        name="cross_attention",
    )(q, kv_all, kv_all)


def _conformer_body(z_ref, zp_ref, w_ref, b_ref, g_ref, beta_ref, o_ref, zs, ys, *, tm, width):
    i = pl.program_id(1)
    zs[pl.ds(CONV_HALO, tm), :] = z_ref[...]
    zs[pl.ds(0, CONV_HALO), :] = jnp.where(i == 0, 0.0, zp_ref[...])
    D = z_ref.shape[-1]
    base = CONV_HALO - (width - 1)
    for c in range(D // LANES):
        sl = slice(c * LANES, (c + 1) * LANES)
        acc = jnp.broadcast_to(b_ref[:, sl], (tm, LANES))
        for k in range(width):
            acc = acc + w_ref[k:k + 1, sl] * zs[pl.ds(base + k, tm), sl]
        ys[:, sl] = acc
    y = ys[...]
    mu = jnp.mean(y, axis=-1, keepdims=True)
    d = y - mu
    var = jnp.mean(d * d, axis=-1, keepdims=True)
    t = d * lax.rsqrt(var + LN_EPS) * g_ref[...] + beta_ref[...]
    o_ref[...] = (t * _sigmoid(t)).astype(o_ref.dtype)


def _conformer_conv(z, dw_w, dw_b, ln_g, ln_b, *, tm=256):
    B, S, D = z.shape
    width = dw_w.shape[0]
    tm = min(tm, S)
    assert width - 1 <= CONV_HALO and tm % CONV_HALO == 0 and S % tm == 0
    r = tm // CONV_HALO
    body = functools.partial(_conformer_body, tm=tm, width=width)
    row = lambda a: a.reshape(1, D)
    return pl.pallas_call(
        body,
        grid=(B, S // tm),
        in_specs=[pl.BlockSpec((None, tm, D), lambda b, i: (b, i, 0)),
                  pl.BlockSpec((None, CONV_HALO, D), lambda b, i: (b, jnp.maximum(i * r - 1, 0), 0)),
                  pl.BlockSpec((width, D), lambda b, i: (0, 0)),
                  pl.BlockSpec((1, D), lambda b, i: (0, 0)),
                  pl.BlockSpec((1, D), lambda b, i: (0, 0)),
                  pl.BlockSpec((1, D), lambda b, i: (0, 0))],
        out_specs=pl.BlockSpec((None, tm, D), lambda b, i: (b, i, 0)),
        out_shape=jax.ShapeDtypeStruct((B, S, D), BF16),
        scratch_shapes=[pltpu.VMEM((tm + CONV_HALO, D), F32), pltpu.VMEM((tm, D), F32)],
        compiler_params=_cparams("parallel", "parallel"),
        name="conformer_conv_ln",
    )(z, z, dw_w, row(dw_b), row(ln_g), row(ln_b))


def _head_sum(x, head_dim):
    n = x.shape[-1]
    r = lax.broadcasted_iota(jnp.int32, (LANES, LANES), 0) // head_dim
    c = lax.broadcasted_iota(jnp.int32, (LANES, LANES), 1) // head_dim
    ones = (r == c).astype(BF16)
    parts = []
    for j in range(n // LANES):
        parts.append(_dot_exact_rhs(x[:, j * LANES:(j + 1) * LANES], ones))
    return parts[0] if len(parts) == 1 else jnp.concatenate(parts, axis=-1)


def _rwkv_prep_body(p_ref, w0_ref, db_ref, a0_ref, ib_ref, gb_ref, kk_ref, ka_ref, rk_ref,
                    r_o, k_o, v_o, kk_o, b_o, lw_o, bonus_o, g_o, *, D):
    lw_pad = LANES
    r = p_ref[:, 0:D]
    k = p_ref[:, D:2 * D]
    v = p_ref[:, 2 * D:3 * D]
    o = 3 * D
    w_lo = p_ref[:, o:o + lw_pad]
    a_lo = p_ref[:, o + lw_pad:o + 2 * lw_pad]
    g_lo = p_ref[:, o + 2 * lw_pad:o + 2 * lw_pad + GATE_LORA]

    w_arg = w0_ref[...] + _dot(jnp.tanh(w_lo).astype(BF16), db_ref[...])
    softplus = jnp.maximum(-w_arg, 0.0) + jnp.log(1.0 + jnp.exp(-jnp.abs(w_arg)))
    lw_o[...] = -jnp.exp(-softplus - 0.5)
    a = _sigmoid(a0_ref[...] + _dot(a_lo.astype(BF16), ib_ref[...]))
    g_o[...] = _dot(_sigmoid(g_lo).astype(BF16), gb_ref[...])
    kk = k * kk_ref[...]
    nrm = jnp.sqrt(_head_sum(kk * kk, RWKV_HEAD_DIM))
    kk = kk / jnp.maximum(nrm, 1e-12)
    k2 = k * (1.0 + (a - 1.0) * ka_ref[...])
    r_o[...] = r
    k_o[...] = k2
    v_o[...] = v
    kk_o[...] = kk
    b_o[...] = kk * a
    bonus_o[...] = _head_sum(r * k2 * rk_ref[...], RWKV_HEAD_DIM) * v


def _rwkv_prep(proj, w0, decay_b, a0, iclr_b, gate_b, k_k, k_a, r_k, *, D, tm=256):
    M, P = proj.shape
    tm = min(tm, M)
    body = functools.partial(_rwkv_prep_body, D=D)
    row = lambda a: a.reshape(1, D)
    full = lambda a: pl.BlockSpec(a.shape, lambda m: (0, 0))
    ins = [proj, row(w0), decay_b, row(a0), iclr_b, gate_b, row(k_k), row(k_a), row(r_k)]
    out_spec = pl.BlockSpec((tm, D), lambda m: (m, 0))
    return pl.pallas_call(
        body,
        grid=(M // tm,),
        in_specs=[pl.BlockSpec((tm, P), lambda m: (m, 0))] + [full(a) for a in ins[1:]],
        out_specs=[out_spec] * 8,
        out_shape=[jax.ShapeDtypeStruct((M, D), F32)] * 8,
        compiler_params=_cparams("parallel"),
        name="rwkv_prep",
    )(*ins)


def _rwkv_scan_body(r_ref, k_ref, v_ref, kk_ref, b_ref, lw_ref, bonus_ref, g_ref, lnw_ref, lnb_ref,
                    o_ref, state, *, C, D):
    c_idx = pl.program_id(1)

    @pl.when(c_idx == 0)
    def _():
        state[...] = jnp.zeros_like(state)

    N = RWKV_HEAD_DIM
    PW = 2 * N
    n_pairs = D // PW
    bf = lambda a: a.astype(BF16)

    row = lax.broadcasted_iota(jnp.int32, (C, C), 0)
    col = lax.broadcasted_iota(jnp.int32, (C, C), 1)
    eye = (col == row).astype(F32)
    r2 = lax.broadcasted_iota(jnp.int32, (2 * C, 2 * C), 0)
    c2 = lax.broadcasted_iota(jnp.int32, (2 * C, 2 * C), 1)
    t2, j2 = r2 % C, c2 % C
    aa_mask = (j2 < t2) | ((r2 >= C) & (j2 == t2))
    lane = lax.broadcasted_iota(jnp.int32, (1, PW), 1)
    head_mask = [lane < N, lane >= N]
    bd_mask = (lax.broadcasted_iota(jnp.int32, (PW, PW), 0) // N) == (lax.broadcasted_iota(jnp.int32, (PW, PW), 1) // N)

    lw = lw_ref[...]
    cum = _dot_exact_lhs((col <= row).astype(BF16), lw)
    cum_last = cum[C - 1:C, :]
    g_inv = jnp.exp(-cum)
    g_tail = jnp.exp(cum_last - cum)
    g_last = jnp.exp(cum_last)
    kf = k_ref[...]
    bfl = b_ref[...]
    r_t = bf(r_ref[...] * jnp.exp(cum))
    kk_t = bf(kk_ref[...] * jnp.exp(cum - lw))
    k_t = bf(kf * g_inv)
    b_t = bf(bfl * g_inv)
    k_h = bf(kf * g_tail)
    b_h = bf(bfl * g_tail)
    v16 = bf(v_ref[...])
    zeros_cv = jnp.zeros((C, PW), BF16)

    heads = [(p, h) for p in range(n_pairs) for h in range(2)]
    psl = lambda p: slice(p * PW, (p + 1) * PW)

    Ls, AkkVs, RKs, Vms = [], [], [], []
    for p, h in heads:
        m = head_mask[h]
        lhs = jnp.concatenate([jnp.where(m, kk_t[:, psl(p)], 0), jnp.where(m, r_t[:, psl(p)], 0)], axis=0)
        rhs = jnp.concatenate([b_t[:, psl(p)], k_t[:, psl(p)]], axis=0)
        aa = jnp.where(aa_mask, _dot_nt(lhs, rhs), 0.0)
        top = aa[0:C, :]
        vm = jnp.where(m, v16[:, psl(p)], 0)
        Ls.append(top[:, 0:C])
        AkkVs.append(_dot(bf(jnp.where(lane >= C, top, 0.0)), jnp.concatenate([zeros_cv, vm], axis=0)))
        RKs.append(bf(aa[C:2 * C, :]))
        Vms.append(vm)

    Ts = [eye - L for L in Ls]
    Ps = Ls
    span = 2
    while span < C:
        Ps = [_dot(bf(P), bf(P)) for P in Ps]
        Ts = [T + _dot(bf(T), bf(P)) for T, P in zip(Ts, Ps)]
        span *= 2

    KRS = []
    for p in range(n_pairs):
        lhs = jnp.concatenate([kk_t[:, psl(p)], r_t[:, psl(p)]], axis=0)
        KRS.append(_dot_nt(lhs, bf(state[p])))
    Es = []
    for i, (p, h) in enumerate(heads):
        rhs = jnp.where(head_mask[h], KRS[p][0:C, :], 0.0) + AkkVs[i]
        Es.append(-_dot(bf(Ts[i]), bf(rhs)))
    ys = []
    for p in range(n_pairs):
        y = KRS[p][C:2 * C, :]
        for h in range(2):
            i = 2 * p + h
            y = y + _dot(RKs[i], jnp.concatenate([bf(Es[i]), Vms[i]], axis=0))
        ys.append(y)
        e16 = bf(Es[2 * p] + Es[2 * p + 1])
        upd = _dot_tn(jnp.concatenate([e16, v16[:, psl(p)]], axis=0),
                      jnp.concatenate([b_h[:, psl(p)], k_h[:, psl(p)]], axis=0))
        state[p] = state[p] * g_last[:, psl(p)] + jnp.where(bd_mask, upd, 0.0)
    y = jnp.concatenate(ys, axis=-1)

    mean = _head_sum(y, N) * (1.0 / N)
    d = y - mean
    var = _head_sum(d * d, N) * (1.0 / N)
    yn = d * lax.rsqrt(var + RWKV_GN_EPS) * lnw_ref[...] + lnb_ref[...] + bonus_ref[...]
    o_ref[...] = (yn * g_ref[...]).astype(o_ref.dtype)


def _rwkv_scan(r, k, v, kk, b, lw, bonus, g, ln_w, ln_b, *, B, S):
    M, D = r.shape
    C = min(RWKV_CHUNK, S)
    nc = S // C
    body = functools.partial(_rwkv_scan_body, C=C, D=D)
    tile = pl.BlockSpec((C, D), lambda bi, c: (bi * nc + c, 0))
    vec = pl.BlockSpec((1, D), lambda bi, c: (0, 0))
    return pl.pallas_call(
        body,
        grid=(B, nc),
        in_specs=[tile] * 8 + [vec, vec],
        out_specs=tile,
        out_shape=jax.ShapeDtypeStruct((M, D), BF16),
        scratch_shapes=[pltpu.VMEM((D // (2 * RWKV_HEAD_DIM), 2 * RWKV_HEAD_DIM, 2 * RWKV_HEAD_DIM), F32)],
        compiler_params=_cparams("parallel", "arbitrary"),
        name="rwkv_scan",
    )(r, k, v, kk, b, lw, bonus, g, ln_w.reshape(1, D), ln_b.reshape(1, D))


def _rmsnorm_body(x_ref, g_ref, o_ref):
    o_ref[...] = _rms_rows(x_ref[...], g_ref[...])


def _rmsnorm(x2d, gain_row, *, tm=512):
    M, K = x2d.shape
    tm = min(tm, M)
    return pl.pallas_call(
        _rmsnorm_body,
        grid=(M // tm,),
        in_specs=[pl.BlockSpec((tm, K), lambda m: (m, 0)), pl.BlockSpec((1, K), lambda m: (0, 0))],
        out_specs=pl.BlockSpec((tm, K), lambda m: (m, 0)),
        out_shape=jax.ShapeDtypeStruct((M, K), F32),
        compiler_params=_cparams("parallel"),
        name="final_rmsnorm",
    )(x2d, gain_row)


def _pad_cols(a, n):
    return jnp.pad(a, ((0, 0), (0, n - a.shape[1])))


def _pad_rows(a, n):
    return jnp.pad(a, ((0, n - a.shape[0]), (0, 0)))


def _fox_layer(x2d, B, S, gain_row, w_in, b_f, q_gain, k_gain, w_o):
    M, D = x2d.shape
    H = FOX_HEADS
    w_main = w_in[:, :4 * D].astype(BF16)
    w_f = _pad_cols(w_in[:, 4 * D:], LANES).astype(BF16)
    b_row = _pad_cols(b_f.reshape(1, H), LANES)
    scale = FOX_HEAD_DIM ** -0.5
    gains = jnp.concatenate([jnp.tile(q_gain * (scale * LOG2E), H), jnp.tile(k_gain, H),
                             jnp.ones((2 * D,), F32)]).reshape(1, 4 * D)
    qkvg = _proj(x2d, gain_row, [w_main], seq=S, epilogue="fox", out_dtype=BF16, extras=(gains,))
    c = _forget_cumsum(x2d.reshape(B, S, D), gain_row, w_f, b_row)
    tk = min(512, S)
    ck_rows = c[:, :, :H].transpose(0, 2, 1).reshape(B, H, S // tk, tk)
    o = _fox_attention(qkvg.reshape(B, S, 4 * D), ck_rows, tk=tk)
    return _matmul_residual(o.reshape(M, D), w_o.astype(BF16), x2d)


def _rwkv_layer(x2d, B, S, gain_row, w_in, mu, w0, decay_b, a0, iclr_b, gate_b, k_k, k_a, r_k,
                ln_w, ln_b, w_o):
    M, D = x2d.shape
    o3 = 3 * D
    segs = [(0, o3, o3), (o3, DECAY_LORA, LANES), (o3 + DECAY_LORA, ICLR_LORA, LANES),
            (o3 + DECAY_LORA + ICLR_LORA, GATE_LORA, GATE_LORA)]
    w_pad = jnp.concatenate([_pad_cols(w_in[:, s:s + n], p) for s, n, p in segs], axis=1)
    mu_pad = jnp.concatenate([_pad_cols(mu[None, s:s + n], p) for s, n, p in segs], axis=1)
    P = w_pad.shape[1]
    tn = 512
    Pp = -(-P // tn) * tn
    w_pad = _pad_cols(w_pad, Pp).astype(BF16)
    mu_pad = _pad_cols(mu_pad, Pp)
    proj = _proj(x2d, gain_row, [w_pad], seq=S, epilogue="lerp", out_dtype=F32, extras=(mu_pad,), tn=tn)
    r, k, v, kk, b, lw, bonus, g = _rwkv_prep(
        proj, w0, _pad_rows(decay_b, LANES).astype(BF16), a0, _pad_rows(iclr_b, LANES).astype(BF16),
        gate_b.astype(BF16), k_k, k_a, r_k.reshape(-1), D=D)
    y = _rwkv_scan(r, k, v, kk, b, lw, bonus, g, ln_w, ln_b, B=B, S=S)
    return _matmul_residual(y, w_o.astype(BF16), x2d)


def _conformer_layer(x2d, B, S, gain_row, w_in, dw_w, dw_b, ln_g, ln_b, w_o):
    M, D = x2d.shape
    tn = 512
    z = _proj(x2d, gain_row, [w_in.astype(BF16)], seq=S, epilogue="glu", out_dtype=F32,
              w_col_offsets=[0, D // tn], tn=tn)
    t = _conformer_conv(z.reshape(B, S, D), dw_w, dw_b, ln_g, ln_b)
    return _matmul_residual(t.reshape(M, D), w_o.astype(BF16), x2d)


def _cross_layer(x2d, B, S, gain_row, kv_all, layer, w_q, w_o):
    M, D = x2d.shape
    scale = (D // MEM_HEADS) ** -0.5
    q = _proj(x2d, gain_row, [w_q.astype(BF16)], seq=S, epilogue="scale_bf16", out_dtype=BF16, scale=scale)
    o = _cross_attention(q.reshape(B, S, D), kv_all, layer)
    return _matmul_residual(o.reshape(M, D), w_o.astype(BF16), x2d)


def _ffn_layer(x2d, S, gain_row, w_up, dw_w, dw_b, w_down):
    d_ff = w_down.shape[0]
    tn = 512
    act = _proj(x2d, gain_row, [w_up.astype(BF16)], seq=S, epilogue="ffn", out_dtype=BF16,
                extras=(dw_w, dw_b.reshape(1, -1)), w_col_offsets=[0, d_ff // tn], tn=tn)
    return _matmul_residual(act, w_down.astype(BF16), x2d)


def kernel(x, mem, mem_norm, norm_mix, norm_cross, norm_ffn, final_norm, fox_w_in, fox_b_f, fox_q_gain, fox_k_gain, fox_w_o, rwkv_w_in, rwkv_mu, rwkv_w0, rwkv_decay_b, rwkv_a0, rwkv_iclr_b, rwkv_gate_b, rwkv_k_k, rwkv_k_a, rwkv_r_k, rwkv_ln_w, rwkv_ln_b, rwkv_w_o, conv_w_in, conv_dw_w, conv_dw_b, conv_ln_g, conv_ln_b, conv_w_o, cross_w_q, cross_w_kv, cross_w_o, ffn_w_up, ffn_dw_w, ffn_dw_b, ffn_w_down):
    B, S, D = x.shape
    depth = norm_mix.shape[0]
    Mem = mem.shape[1]
    M = B * S
    x2d = x.reshape(M, D)

    w_kv_all = jnp.concatenate([cross_w_kv[i] for i in range(depth)], axis=1).astype(BF16)
    kv_all = _proj(mem.reshape(B * Mem, D), mem_norm.reshape(1, D), [w_kv_all], seq=Mem,
                   epilogue="scale_bf16", out_dtype=BF16).reshape(B, Mem, depth * 2 * D)

    for i in range(depth):
        kind, j = i % 3, i // 3
        g_mix = norm_mix[i].reshape(1, D)
        if kind == 0:
            x2d = _fox_layer(x2d, B, S, g_mix, fox_w_in[j], fox_b_f[j], fox_q_gain[j], fox_k_gain[j],
                             fox_w_o[j])
        elif kind == 1:
            x2d = _rwkv_layer(x2d, B, S, g_mix, rwkv_w_in[j], rwkv_mu[j], rwkv_w0[j], rwkv_decay_b[j],
                              rwkv_a0[j], rwkv_iclr_b[j], rwkv_gate_b[j], rwkv_k_k[j], rwkv_k_a[j],
                              rwkv_r_k[j], rwkv_ln_w[j], rwkv_ln_b[j], rwkv_w_o[j])
        else:
            x2d = _conformer_layer(x2d, B, S, g_mix, conv_w_in[j], conv_dw_w[j], conv_dw_b[j],
                                   conv_ln_g[j], conv_ln_b[j], conv_w_o[j])
        x2d = _cross_layer(x2d, B, S, norm_cross[i].reshape(1, D), kv_all, i, cross_w_q[i], cross_w_o[i])
        x2d = _ffn_layer(x2d, S, norm_ffn[i].reshape(1, D), ffn_w_up[i], ffn_dw_w[i], ffn_dw_b[i],
                         ffn_w_down[i])
    return _rmsnorm(x2d, final_norm.reshape(1, D)).reshape(B, S, D)
```

```python
import functools

import jax
import jax.numpy as jnp
from jax import lax
from jax.experimental import pallas as pl
from jax.experimental.pallas import tpu as pltpu

F32 = jnp.float32
BF16 = jnp.bfloat16

RMS_EPS = 1e-6
LN_EPS = 1e-5
RWKV_GN_EPS = 64e-5

LANES = 128
HALO = 16
SUB_COLS = 256
VMEM_LIMIT = 56 * 1024 * 1024

FOX_HEADS = 16
FOX_HEAD_DIM = 128
MEM_HEADS = 4
RWKV_HEAD_DIM = 64
RWKV_CHUNK = 64
RWKV_GROUP_HEADS = 4
DECAY_LORA = 96
ICLR_LORA = 96
GATE_LORA = 256
CONV_HALO = 32
NEG_BIG = -1e30
LOG2E = 1.4426950408889634


def _cparams(*sem):
    return pltpu.CompilerParams(dimension_semantics=sem, vmem_limit_bytes=VMEM_LIMIT)


def _rms_rows(x, g):
    ms = jnp.mean(x * x, axis=-1, keepdims=True)
    return x * lax.rsqrt(ms + RMS_EPS) * g


def _sigmoid(x):
    return 1.0 / (1.0 + jnp.exp(-x))


def _dot(a, b):
    return jnp.dot(a, b, preferred_element_type=F32)


def _dot_nt(a, b):
    return lax.dot_general(a, b, (((1,), (1,)), ((), ())), preferred_element_type=F32)


def _dot_tn(a, b):
    return lax.dot_general(a, b, (((0,), (0,)), ((), ())), preferred_element_type=F32)


def _split3(x):
    hi = x.astype(BF16)
    r1 = x - hi.astype(F32)
    mid = r1.astype(BF16)
    lo = (r1 - mid.astype(F32)).astype(BF16)
    return hi, mid, lo


def _dot_exact_lhs(sel, x):
    hi, mid, lo = _split3(x)
    return _dot(sel, hi) + _dot(sel, mid) + _dot(sel, lo)


def _dot_exact_rhs(x, sel):
    hi, mid, lo = _split3(x)
    return _dot(hi, sel) + _dot(mid, sel) + _dot(lo, sel)


def _proj_body(*refs, n_w, halo, epilogue, tm, tn, tiles_per_seq, scale, n_row_tiles, n_col_tiles):
    it = iter(refs)
    x_ref = next(it)
    xp_ref = next(it) if halo else None
    g_ref = next(it)
    w_refs = [next(it) for _ in range(n_w)]
    if epilogue == "ffn":
        cw_refs = [next(it) for _ in range(n_w)]
        cb_refs = [next(it) for _ in range(n_w)]
    elif epilogue in ("lerp", "head_norm"):
        vec_ref = next(it)
    out_ref = next(it)
    h_scr = next(it)
    acc_scr = next(it)

    t = pl.program_id(0)
    total = n_row_tiles * n_col_tiles
    n = t % n_col_tiles
    m = jnp.minimum(t // n_col_tiles, n_row_tiles - 1)
    off = HALO if halo else 0

    @pl.when(t == 0)
    def _():
        acc_scr[1] = jnp.zeros(acc_scr.shape[1:], F32)

    @pl.when((n == 0) & (t < total))
    def _():
        g = g_ref[...]
        h_scr[pl.ds(off, tm), :] = _rms_rows(x_ref[...], g).astype(BF16)
        if halo:
            hp = _rms_rows(xp_ref[...], g)
            first = (m % tiles_per_seq) == 0
            h_scr[pl.ds(0, HALO), :] = jnp.where(first, 0.0, hp).astype(BF16)

    sub = min(SUB_COLS, tn)
    col_slices = [slice(c * sub, (c + 1) * sub) for c in range(tn // sub)]

    def store(cs, val):
        out_ref[:, cs] = val.astype(out_ref.dtype)

    def step(slot):
        def prev_rows(wi, cs, shift=0):
            return acc_scr[1 - slot, wi, pl.ds(off - shift, tm), cs]

        for cs in col_slices:
            if epilogue == "scale":
                store(cs, prev_rows(0, cs) * scale)
            elif epilogue == "sigmoid":
                store(cs, _sigmoid(prev_rows(0, cs)))
            elif epilogue == "glu":
                store(cs, prev_rows(0, cs) * _sigmoid(prev_rows(1, cs)))
            elif epilogue == "head_norm":
                for j in range(sub // FOX_HEAD_DIM):
                    lo = cs.start + j * FOX_HEAD_DIM
                    sl = slice(lo, lo + FOX_HEAD_DIM)
                    store(sl, _rms_rows(prev_rows(0, sl), vec_ref[:, sl]))
            elif epilogue == "ffn":
                ys = []
                for wi, (cw_ref, cb_ref) in enumerate(zip(cw_refs, cb_refs)):
                    cw = cw_ref[:, cs]
                    taps = cw.shape[0]
                    y = cb_ref[:, cs] + cw[taps - 1:taps, :] * prev_rows(wi, cs)
                    for j in range(taps - 1):
                        y = y + cw[j:j + 1, :] * prev_rows(wi, cs, shift=taps - 1 - j)
                    ys.append(y)
                u, gte = ys
                store(cs, gte * _sigmoid(gte) * u)
            elif epilogue == "lerp":
                cur = prev_rows(0, cs)
                store(cs, cur + (prev_rows(0, cs, shift=1) - cur) * vec_ref[:, cs])
            else:
                raise ValueError(epilogue)

        h = h_scr[...]
        for cs in col_slices:
            for wi, w_ref in enumerate(w_refs):
                acc_scr[slot, wi, :, cs] = _dot(h, w_ref[:, cs])

    pl.when(t % 2 == 0)(functools.partial(step, 0))
    pl.when(t % 2 == 1)(functools.partial(step, 1))


def _proj(x2d, gain_row, ws, *, seq, epilogue, out_dtype, extras=(), scale=1.0, tm=512, tn=512,
          w_col_offsets=None):
    M, K = x2d.shape
    n_w = len(ws) if w_col_offsets is None else len(w_col_offsets)
    halo = epilogue in ("ffn", "lerp")
    tm = min(tm, M, seq)
    assert M % tm == 0 and seq % tm == 0 and tm % HALO == 0
    if w_col_offsets is None:
        w_col_offsets = [0] * n_w
        n_out = ws[0].shape[1]
        w_list = list(ws)
    else:
        w_list = [ws[0]] * n_w
        n_out = ws[0].shape[1] // n_w
    tn = min(tn, n_out)
    assert n_out % tn == 0
    MT, NT = M // tm, n_out // tn

    cur_m = lambda t: jnp.minimum(t // NT, MT - 1)
    cur_n = lambda t: t % NT
    prev_m = lambda t: jnp.maximum(t - 1, 0) // NT
    prev_n = lambda t: jnp.maximum(t - 1, 0) % NT

    in_specs = [pl.BlockSpec((tm, K), lambda t: (cur_m(t), 0))]
    args = [x2d]
    if halo:
        r = tm // HALO
        in_specs.append(pl.BlockSpec((HALO, K), lambda t: (jnp.maximum(cur_m(t) * r - 1, 0), 0)))
        args.append(x2d)
    in_specs.append(pl.BlockSpec((1, K), lambda t: (0, 0)))
    args.append(gain_row)
    for w, o in zip(w_list, w_col_offsets):
        in_specs.append(pl.BlockSpec((K, tn), functools.partial(lambda t, o: (0, cur_n(t) + o), o=o)))
        args.append(w)
    if epilogue == "ffn":
        cw, cb = extras
        taps = cw.shape[0]
        assert taps - 1 <= HALO
        for o in w_col_offsets:
            in_specs.append(pl.BlockSpec((taps, tn), functools.partial(lambda t, o: (0, prev_n(t) + o), o=o)))
            args.append(cw)
        for o in w_col_offsets:
            in_specs.append(pl.BlockSpec((1, tn), functools.partial(lambda t, o: (0, prev_n(t) + o), o=o)))
            args.append(cb)
    elif epilogue in ("lerp", "head_norm"):
        in_specs.append(pl.BlockSpec((1, tn), lambda t: (0, prev_n(t))))
        args.append(extras[0])

    rows = tm + (HALO if halo else 0)
    scratch = [pltpu.VMEM((rows, K), BF16), pltpu.VMEM((2, n_w, rows, tn), F32)]

    body = functools.partial(_proj_body, n_w=n_w, halo=halo, epilogue=epilogue, tm=tm, tn=tn,
                             tiles_per_seq=seq // tm, scale=scale, n_row_tiles=MT, n_col_tiles=NT)
    return pl.pallas_call(
        body,
        grid=(MT * NT + 1,),
        in_specs=in_specs,
        out_specs=pl.BlockSpec((tm, tn), lambda t: (prev_m(t), prev_n(t))),
        out_shape=jax.ShapeDtypeStruct((M, n_out), out_dtype),
        scratch_shapes=scratch,
        compiler_params=_cparams("arbitrary"),
        name="proj_" + epilogue,
    )(*args)


def _mm_res_body(a_ref, w_ref, r_ref, o_ref):
    a = a_ref[...]
    sub = min(SUB_COLS, o_ref.shape[-1])
    for c in range(o_ref.shape[-1] // sub):
        cs = slice(c * sub, (c + 1) * sub)
        o_ref[:, cs] = r_ref[:, cs] + _dot(a, w_ref[:, cs])


def _matmul_residual(a, w, res, *, tn=1024):
    M, K = a.shape
    N = w.shape[1]
    tm = min(1024 if K <= 2048 else 512, M)
    tn = min(tn, N)
    assert M % tm == 0 and N % tn == 0
    return pl.pallas_call(
        _mm_res_body,
        grid=(M // tm, N // tn),
        in_specs=[pl.BlockSpec((tm, K), lambda m, n: (m, 0)),
                  pl.BlockSpec((K, tn), lambda m, n: (0, n)),
                  pl.BlockSpec((tm, tn), lambda m, n: (m, n))],
        out_specs=pl.BlockSpec((tm, tn), lambda m, n: (m, n)),
        out_shape=jax.ShapeDtypeStruct((M, N), F32),
        compiler_params=_cparams("parallel", "parallel"),
        name="matmul_residual",
    )(a, w, res)


def _forget_body(x_ref, g_ref, w_ref, b_ref, c_ref, carry):
    s = pl.program_id(1)

    @pl.when(s == 0)
    def _():
        carry[...] = jnp.zeros_like(carry)

    h = _rms_rows(x_ref[...], g_ref[...]).astype(BF16)
    z = _dot(h, w_ref[...]) + b_ref[...]
    logf = jnp.minimum(z, 0.0) - jnp.log(1.0 + jnp.exp(-jnp.abs(z)))
    tm = logf.shape[0]
    row = lax.broadcasted_iota(jnp.int32, (tm, tm), 0)
    col = lax.broadcasted_iota(jnp.int32, (tm, tm), 1)
    tri = (col <= row).astype(BF16)
    c = _dot_exact_lhs(tri, logf) + carry[...]
    c_ref[...] = c * LOG2E
    carry[...] = c[tm - 1:tm, :]


def _forget_cumsum(x, gain_row, w_f, b_f, *, tm=256):
    B, S, K = x.shape
    tm = min(tm, S)
    return pl.pallas_call(
        _forget_body,
        grid=(B, S // tm),
        in_specs=[pl.BlockSpec((None, tm, K), lambda b, s: (b, s, 0)),
                  pl.BlockSpec((1, K), lambda b, s: (0, 0)),
                  pl.BlockSpec((K, LANES), lambda b, s: (0, 0)),
                  pl.BlockSpec((1, LANES), lambda b, s: (0, 0))],
        out_specs=pl.BlockSpec((None, tm, LANES), lambda b, s: (b, s, 0)),
        out_shape=jax.ShapeDtypeStruct((B, S, LANES), F32),
        scratch_shapes=[pltpu.VMEM((1, LANES), F32)],
        compiler_params=_cparams("parallel", "arbitrary"),
        name="fox_forget_cumsum",
    )(x, gain_row, w_f, b_f)


def _fox_attn_body(q_ref, k_ref, v_ref, gate_ref, ck_ref, o_ref, m_scr, l_scr, acc_scr, *, tq, tk, heads):
    i = pl.program_id(2)
    hd = FOX_HEAD_DIM
    rep = tk // LANES
    m_scr[...] = jnp.full_like(m_scr, NEG_BIG)
    l_scr[...] = jnp.zeros_like(l_scr)
    acc_scr[...] = jnp.zeros_like(acc_scr)

    def block(j, masked):
        start = pl.multiple_of(j * tk, tk)
        if masked:
            qpos = i * tq + lax.broadcasted_iota(jnp.int32, (tq, tk), 0)
            kpos = j * tk + lax.broadcasted_iota(jnp.int32, (tq, tk), 1)
            visible = kpos <= qpos
        for h in range(heads):
            sl = slice(h * hd, (h + 1) * hd)
            s = _dot_nt(q_ref[:, sl], k_ref[pl.ds(start, tk), sl]) - ck_ref[h, pl.ds(j, 1), :]
            if masked:
                s = jnp.where(visible, s, NEG_BIG)
            m_old = m_scr[h]
            m_new = jnp.maximum(m_old, jnp.max(s, axis=-1, keepdims=True))
            alpha = jnp.exp2(m_old - m_new)
            p = jnp.exp2(s - jnp.tile(m_new, (1, rep)))
            l_scr[h] = alpha * l_scr[h] + jnp.sum(p, axis=-1, keepdims=True)
            acc_scr[h] = alpha * acc_scr[h] + _dot(p.astype(BF16), v_ref[pl.ds(start, tk), sl])
            m_scr[h] = m_new

    n_full = (i * tq) // tk

    def pair_step(jj, c):
        block(2 * jj, False)
        block(2 * jj + 1, False)
        return c

    lax.fori_loop(0, n_full // 2, pair_step, 0)

    @pl.when(n_full % 2 == 1)
    def _():
        block(n_full - 1, False)

    block(n_full, True)
    for h in range(heads):
        sl = slice(h * hd, (h + 1) * hd)
        o = acc_scr[h] / l_scr[h]
        o_ref[:, sl] = (o * gate_ref[:, sl].astype(F32)).astype(o_ref.dtype)


def _fox_attention(qk, v, gate, ck_rows, *, tq=512, tk=512, heads=2):
    B, S, D = v.shape
    H = D // FOX_HEAD_DIM
    HG = H // heads
    W = heads * FOX_HEAD_DIM
    tk = min(tk, S)
    tq = min(tq, tk)
    assert tk % tq == 0 and S % tk == 0 and H % heads == 0
    body = functools.partial(_fox_attn_body, tq=tq, tk=tk, heads=heads)
    return pl.pallas_call(
        body,
        grid=(B, HG, S // tq),
        in_specs=[pl.BlockSpec((None, tq, W), lambda b, h, i: (b, i, h)),
                  pl.BlockSpec((None, S, W), lambda b, h, i: (b, 0, HG + h)),
                  pl.BlockSpec((None, S, W), lambda b, h, i: (b, 0, h)),
                  pl.BlockSpec((None, tq, W), lambda b, h, i: (b, i, h)),
                  pl.BlockSpec((None, heads, S // tk, tk), lambda b, h, i: (b, h, 0, 0))],
        out_specs=pl.BlockSpec((None, tq, W), lambda b, h, i: (b, i, h)),
        out_shape=jax.ShapeDtypeStruct((B, S, D), BF16),
        scratch_shapes=[pltpu.VMEM((heads, tq, LANES), F32), pltpu.VMEM((heads, tq, LANES), F32),
                        pltpu.VMEM((heads, tq, FOX_HEAD_DIM), F32)],
        compiler_params=_cparams("parallel", "parallel", "arbitrary"),
        name="fox_attention",
    )(qk, qk, v, gate, ck_rows)


def _cross_attn_body(q_ref, k_ref, v_ref, o_ref, *, heads):
    dh = q_ref.shape[-1] // heads
    for h in range(heads):
        sl = slice(h * dh, (h + 1) * dh)
        s = _dot_nt(q_ref[:, sl], k_ref[:, sl])
        s = s - jnp.max(s, axis=-1, keepdims=True)
        p = jnp.exp(s)
        l = jnp.sum(p, axis=-1, keepdims=True)
        o = _dot(p.astype(BF16), v_ref[:, sl])
        o_ref[:, sl] = (o / l).astype(o_ref.dtype)


def _cross_attention(q, kv_all, layer, *, tq=512):
    B, S, D = q.shape
    Mem = kv_all.shape[1]
    tq = min(tq, S)
    body = functools.partial(_cross_attn_body, heads=MEM_HEADS)
    return pl.pallas_call(
        body,
        grid=(B, S // tq),
        in_specs=[pl.BlockSpec((None, tq, D), lambda b, i: (b, i, 0)),
                  pl.BlockSpec((None, Mem, D), lambda b, i: (b, 0, 2 * layer)),
                  pl.BlockSpec((None, Mem, D), lambda b, i: (b, 0, 2 * layer + 1))],
        out_specs=pl.BlockSpec((None, tq, D), lambda b, i: (b, i, 0)),
        out_shape=jax.ShapeDtypeStruct((B, S, D), BF16),
        compiler_params=_cparams("parallel", "parallel"),
        name="cross_attention",
    )(q, kv_all, kv_all)


def _conformer_body(z_ref, zp_ref, w_ref, b_ref, g_ref, beta_ref, o_ref, zs, ys, *, tm, width):
    i = pl.program_id(1)
    zs[pl.ds(CONV_HALO, tm), :] = z_ref[...]
    zs[pl.ds(0, CONV_HALO), :] = jnp.where(i == 0, 0.0, zp_ref[...])
    D = z_ref.shape[-1]
    base = CONV_HALO - (width - 1)
    for c in range(D // LANES):
        sl = slice(c * LANES, (c + 1) * LANES)
        acc = jnp.broadcast_to(b_ref[:, sl], (tm, LANES))
        for k in range(width):
            acc = acc + w_ref[k:k + 1, sl] * zs[pl.ds(base + k, tm), sl]
        ys[:, sl] = acc
    y = ys[...]
    mu = jnp.mean(y, axis=-1, keepdims=True)
    d = y - mu
    var = jnp.mean(d * d, axis=-1, keepdims=True)
    t = d * lax.rsqrt(var + LN_EPS) * g_ref[...] + beta_ref[...]
    o_ref[...] = (t * _sigmoid(t)).astype(o_ref.dtype)


def _conformer_conv(z, dw_w, dw_b, ln_g, ln_b, *, tm=256):
    B, S, D = z.shape
    width = dw_w.shape[0]
    tm = min(tm, S)
    assert width - 1 <= CONV_HALO and tm % CONV_HALO == 0 and S % tm == 0
    r = tm // CONV_HALO
    body = functools.partial(_conformer_body, tm=tm, width=width)
    row = lambda a: a.reshape(1, D)
    return pl.pallas_call(
        body,
        grid=(B, S // tm),
        in_specs=[pl.BlockSpec((None, tm, D), lambda b, i: (b, i, 0)),
                  pl.BlockSpec((None, CONV_HALO, D), lambda b, i: (b, jnp.maximum(i * r - 1, 0), 0)),
                  pl.BlockSpec((width, D), lambda b, i: (0, 0)),
                  pl.BlockSpec((1, D), lambda b, i: (0, 0)),
                  pl.BlockSpec((1, D), lambda b, i: (0, 0)),
                  pl.BlockSpec((1, D), lambda b, i: (0, 0))],
        out_specs=pl.BlockSpec((None, tm, D), lambda b, i: (b, i, 0)),
        out_shape=jax.ShapeDtypeStruct((B, S, D), BF16),
        scratch_shapes=[pltpu.VMEM((tm + CONV_HALO, D), F32), pltpu.VMEM((tm, D), F32)],
        compiler_params=_cparams("parallel", "parallel"),
        name="conformer_conv_ln",
    )(z, z, dw_w, row(dw_b), row(ln_g), row(ln_b))


def _head_sum(x, head_dim):
    n = x.shape[-1]
    r = lax.broadcasted_iota(jnp.int32, (LANES, LANES), 0) // head_dim
    c = lax.broadcasted_iota(jnp.int32, (LANES, LANES), 1) // head_dim
    ones = (r == c).astype(BF16)
    parts = []
    for j in range(n // LANES):
        parts.append(_dot_exact_rhs(x[:, j * LANES:(j + 1) * LANES], ones))
    return parts[0] if len(parts) == 1 else jnp.concatenate(parts, axis=-1)


def _rwkv_prep_body(p_ref, w0_ref, db_ref, a0_ref, ib_ref, gb_ref, kk_ref, ka_ref, rk_ref,
                    r_o, k_o, v_o, kk_o, b_o, lw_o, bonus_o, g_o, *, D):
    lw_pad = LANES
    r = p_ref[:, 0:D]
    k = p_ref[:, D:2 * D]
    v = p_ref[:, 2 * D:3 * D]
    o = 3 * D
    w_lo = p_ref[:, o:o + lw_pad]
    a_lo = p_ref[:, o + lw_pad:o + 2 * lw_pad]
    g_lo = p_ref[:, o + 2 * lw_pad:o + 2 * lw_pad + GATE_LORA]

    w_arg = w0_ref[...] + _dot(jnp.tanh(w_lo).astype(BF16), db_ref[...])
    softplus = jnp.maximum(-w_arg, 0.0) + jnp.log(1.0 + jnp.exp(-jnp.abs(w_arg)))
    lw_o[...] = -jnp.exp(-softplus - 0.5)
    a = _sigmoid(a0_ref[...] + _dot(a_lo.astype(BF16), ib_ref[...]))
    g_o[...] = _dot(_sigmoid(g_lo).astype(BF16), gb_ref[...])
    kk = k * kk_ref[...]
    nrm = jnp.sqrt(_head_sum(kk * kk, RWKV_HEAD_DIM))
    kk = kk / jnp.maximum(nrm, 1e-12)
    k2 = k * (1.0 + (a - 1.0) * ka_ref[...])
    r_o[...] = r
    k_o[...] = k2
    v_o[...] = v
    kk_o[...] = kk
    b_o[...] = kk * a
    bonus_o[...] = _head_sum(r * k2 * rk_ref[...], RWKV_HEAD_DIM) * v


def _rwkv_prep(proj, w0, decay_b, a0, iclr_b, gate_b, k_k, k_a, r_k, *, D, tm=256):
    M, P = proj.shape
    tm = min(tm, M)
    body = functools.partial(_rwkv_prep_body, D=D)
    row = lambda a: a.reshape(1, D)
    full = lambda a: pl.BlockSpec(a.shape, lambda m: (0, 0))
    ins = [proj, row(w0), decay_b, row(a0), iclr_b, gate_b, row(k_k), row(k_a), row(r_k)]
    out_spec = pl.BlockSpec((tm, D), lambda m: (m, 0))
    return pl.pallas_call(
        body,
        grid=(M // tm,),
        in_specs=[pl.BlockSpec((tm, P), lambda m: (m, 0))] + [full(a) for a in ins[1:]],
        out_specs=[out_spec] * 8,
        out_shape=[jax.ShapeDtypeStruct((M, D), F32)] * 8,
        compiler_params=_cparams("parallel"),
        name="rwkv_prep",
    )(*ins)


def _rwkv_scan_body(r_ref, k_ref, v_ref, kk_ref, b_ref, lw_ref, bonus_ref, g_ref, lnw_ref, lnb_ref,
                    o_ref, state, *, C, D):
    c_idx = pl.program_id(0)

    @pl.when(c_idx == 0)
    def _():
        state[...] = jnp.zeros_like(state)

    N = RWKV_HEAD_DIM
    GH = RWKV_GROUP_HEADS
    GW = GH * N
    n_groups = D // GW
    n_batch = r_ref.shape[0]
    bf = lambda a: a.astype(BF16)

    row = lax.broadcasted_iota(jnp.int32, (C, GW), 0)
    lane = lax.broadcasted_iota(jnp.int32, (C, GW), 1)
    strict = (lane % N) < row
    incl = (lane % N) <= row
    eye = ((lane % N) == row).astype(F32)
    head_of_lane = lax.broadcasted_iota(jnp.int32, (1, GW), 1) // N
    bd_mask = (lax.broadcasted_iota(jnp.int32, (GW, GW), 0) // N) == (lax.broadcasted_iota(jnp.int32, (GW, GW), 1) // N)
    tri = (lax.broadcasted_iota(jnp.int32, (C, C), 1) <= lax.broadcasted_iota(jnp.int32, (C, C), 0)).astype(BF16)

    def block_diag(x):
        return jnp.concatenate([jnp.where(head_of_lane == h, x, jnp.zeros_like(x)) for h in range(GH)], axis=0)

    units = [(bi, g) for bi in range(n_batch) for g in range(n_groups)]
    gsl = lambda g: slice(g * GW, (g + 1) * GW)

    pre = []
    for bi in range(n_batch):
        lw = lw_ref[bi]
        cum = _dot_exact_lhs(tri, lw)
        cum_last = cum[C - 1:C, :]
        g_inv = jnp.exp(-cum)
        g_tail = jnp.exp(cum_last - cum)
        kf = k_ref[bi]
        bfl = b_ref[bi]
        pre.append(dict(
            r_t=bf(r_ref[bi] * jnp.exp(cum)), kk_t=bf(kk_ref[bi] * jnp.exp(cum - lw)),
            k_t=bf(kf * g_inv), b_t=bf(bfl * g_inv), k_h=bf(kf * g_tail), b_h=bf(bfl * g_tail),
            v=bf(v_ref[bi]), g_last=jnp.exp(cum_last)))

    lhs_kr, Ls, Akks, RKs, Vbd = [], [], [], [], []
    for bi, g in units:
        p = pre[bi]
        lhs = jnp.concatenate([p["kk_t"][:, gsl(g)], p["r_t"][:, gsl(g)]], axis=0)
        ab = _dot_nt(lhs, block_diag(p["b_t"][:, gsl(g)]))
        ak = _dot_nt(lhs, block_diag(p["k_t"][:, gsl(g)]))
        lhs_kr.append(lhs)
        Ls.append(jnp.where(strict, ab[0:C], 0.0))
        Akks.append(bf(jnp.where(strict, ak[0:C], 0.0)))
        RKs.append(jnp.concatenate([bf(jnp.where(incl, ab[C:2 * C], 0.0)),
                                    bf(jnp.where(incl, ak[C:2 * C], 0.0))], axis=1))
        Vbd.append(block_diag(p["v"][:, gsl(g)]))
    AkkVs = [_dot(a, vb) for a, vb in zip(Akks, Vbd)]

    Xs = [eye - L for L in Ls]
    Qs = [_dot(bf(L), block_diag(bf(L))) for L in Ls]
    span = 4
    while span < C:
        nxt = [_dot(jnp.concatenate([bf(Q), bf(X)], axis=0), block_diag(bf(Q))) for Q, X in zip(Qs, Xs)]
        Qs = [n[0:C] for n in nxt]
        Xs = [X + n[C:2 * C] for X, n in zip(Xs, nxt)]
        span *= 2
    Xs = [X + _dot(bf(X), block_diag(bf(Q))) for Q, X in zip(Qs, Xs)]

    KRS = [_dot_nt(lhs, bf(state[bi, g])) for lhs, (bi, g) in zip(lhs_kr, units)]
    Es = [-_dot(bf(X), block_diag(bf(krs[0:C] + akkv))) for X, krs, akkv in zip(Xs, KRS, AkkVs)]
    ys = [[None] * n_groups for _ in range(n_batch)]
    for i, (bi, g) in enumerate(units):
        p = pre[bi]
        e16 = bf(Es[i])
        ys[bi][g] = KRS[i][C:2 * C] + _dot(RKs[i], jnp.concatenate([block_diag(e16), Vbd[i]], axis=0))
        upd = _dot_tn(jnp.concatenate([e16, p["v"][:, gsl(g)]], axis=0),
                      jnp.concatenate([p["b_h"][:, gsl(g)], p["k_h"][:, gsl(g)]], axis=0))
        state[bi, g] = state[bi, g] * p["g_last"][:, gsl(g)] + jnp.where(bd_mask, upd, 0.0)

    for bi in range(n_batch):
        y = jnp.concatenate(ys[bi], axis=-1)
        mean = _head_sum(y, N) * (1.0 / N)
        d = y - mean
        var = _head_sum(d * d, N) * (1.0 / N)
        yn = d * lax.rsqrt(var + RWKV_GN_EPS) * lnw_ref[...] + lnb_ref[...] + bonus_ref[bi]
        o_ref[bi] = (yn * g_ref[bi]).astype(o_ref.dtype)


def _rwkv_scan(r, k, v, kk, b, lw, bonus, g, ln_w, ln_b, *, B, S):
    M, D = r.shape
    C = RWKV_CHUNK
    assert S % C == 0 and C == RWKV_HEAD_DIM and D % (RWKV_GROUP_HEADS * RWKV_HEAD_DIM) == 0
    nc = S // C
    GW = RWKV_GROUP_HEADS * RWKV_HEAD_DIM
    body = functools.partial(_rwkv_scan_body, C=C, D=D)
    tile = pl.BlockSpec((B, C, D), lambda c: (0, c, 0))
    vec = pl.BlockSpec((1, D), lambda c: (0, 0))
    seq = lambda a: a.reshape(B, S, D)
    return pl.pallas_call(
        body,
        grid=(nc,),
        in_specs=[tile] * 8 + [vec, vec],
        out_specs=tile,
        out_shape=jax.ShapeDtypeStruct((B, S, D), BF16),
        scratch_shapes=[pltpu.VMEM((B, D // GW, GW, GW), F32)],
        compiler_params=_cparams("arbitrary"),
        name="rwkv_scan",
    )(*(seq(a) for a in (r, k, v, kk, b, lw, bonus, g)), ln_w.reshape(1, D), ln_b.reshape(1, D)).reshape(M, D)


def _rmsnorm_body(x_ref, g_ref, o_ref):
    o_ref[...] = _rms_rows(x_ref[...], g_ref[...])


def _rmsnorm(x2d, gain_row, *, tm=512):
    M, K = x2d.shape
    tm = min(tm, M)
    return pl.pallas_call(
        _rmsnorm_body,
        grid=(M // tm,),
        in_specs=[pl.BlockSpec((tm, K), lambda m: (m, 0)), pl.BlockSpec((1, K), lambda m: (0, 0))],
        out_specs=pl.BlockSpec((tm, K), lambda m: (m, 0)),
        out_shape=jax.ShapeDtypeStruct((M, K), F32),
        compiler_params=_cparams("parallel"),
        name="final_rmsnorm",
    )(x2d, gain_row)


def _pad_cols(a, n):
    return jnp.pad(a, ((0, 0), (0, n - a.shape[1])))


def _pad_rows(a, n):
    return jnp.pad(a, ((0, n - a.shape[0]), (0, 0)))


def _fox_layer(x2d, B, S, gain_row, w_in, b_f, q_gain, k_gain, w_o):
    M, D = x2d.shape
    H = FOX_HEADS
    w_f = _pad_cols(w_in[:, 4 * D:], LANES).astype(BF16)
    b_row = _pad_cols(b_f.reshape(1, H), LANES)
    scale = FOX_HEAD_DIM ** -0.5
    gains = jnp.concatenate([jnp.tile(q_gain * (scale * LOG2E), H), jnp.tile(k_gain, H)]).reshape(1, 2 * D)
    proj = functools.partial(_proj, x2d, gain_row, seq=S, out_dtype=BF16, tn=1024)
    qk = proj([w_in[:, :2 * D].astype(BF16)], epilogue="head_norm", extras=(gains,))
    v = proj([w_in[:, 2 * D:3 * D].astype(BF16)], epilogue="scale")
    gate = proj([w_in[:, 3 * D:4 * D].astype(BF16)], epilogue="sigmoid")
    c = _forget_cumsum(x2d.reshape(B, S, D), gain_row, w_f, b_row)
    tk = min(512, S)
    ck_rows = c[:, :, :H].transpose(0, 2, 1).reshape(B, H, S // tk, tk)
    o = _fox_attention(qk.reshape(B, S, 2 * D), v.reshape(B, S, D), gate.reshape(B, S, D), ck_rows, tk=tk)
    return _matmul_residual(o.reshape(M, D), w_o.astype(BF16), x2d)


def _rwkv_layer(x2d, B, S, gain_row, w_in, mu, w0, decay_b, a0, iclr_b, gate_b, k_k, k_a, r_k,
                ln_w, ln_b, w_o):
    M, D = x2d.shape
    o3 = 3 * D
    segs = [(0, o3, o3), (o3, DECAY_LORA, LANES), (o3 + DECAY_LORA, ICLR_LORA, LANES),
            (o3 + DECAY_LORA + ICLR_LORA, GATE_LORA, GATE_LORA)]
    w_pad = jnp.concatenate([_pad_cols(w_in[:, s:s + n], p) for s, n, p in segs], axis=1)
    mu_pad = jnp.concatenate([_pad_cols(mu[None, s:s + n], p) for s, n, p in segs], axis=1)
    P = w_pad.shape[1]
    tn = 512
    Pp = -(-P // tn) * tn
    w_pad = _pad_cols(w_pad, Pp).astype(BF16)
    mu_pad = _pad_cols(mu_pad, Pp)
    proj = _proj(x2d, gain_row, [w_pad], seq=S, epilogue="lerp", out_dtype=F32, extras=(mu_pad,), tn=tn)
    r, k, v, kk, b, lw, bonus, g = _rwkv_prep(
        proj, w0, _pad_rows(decay_b, LANES).astype(BF16), a0, _pad_rows(iclr_b, LANES).astype(BF16),
        gate_b.astype(BF16), k_k, k_a, r_k.reshape(-1), D=D)
    y = _rwkv_scan(r, k, v, kk, b, lw, bonus, g, ln_w, ln_b, B=B, S=S)
    return _matmul_residual(y, w_o.astype(BF16), x2d)


def _conformer_layer(x2d, B, S, gain_row, w_in, dw_w, dw_b, ln_g, ln_b, w_o):
    M, D = x2d.shape
    tn = 512
    z = _proj(x2d, gain_row, [w_in.astype(BF16)], seq=S, epilogue="glu", out_dtype=F32,
              w_col_offsets=[0, D // tn], tn=tn)
    t = _conformer_conv(z.reshape(B, S, D), dw_w, dw_b, ln_g, ln_b)
    return _matmul_residual(t.reshape(M, D), w_o.astype(BF16), x2d)


def _cross_layer(x2d, B, S, gain_row, kv_all, layer, w_q, w_o):
    M, D = x2d.shape
    scale = (D // MEM_HEADS) ** -0.5
    q = _proj(x2d, gain_row, [w_q.astype(BF16)], seq=S, epilogue="scale", out_dtype=BF16, scale=scale, tn=1024)
    o = _cross_attention(q.reshape(B, S, D), kv_all, layer)
    return _matmul_residual(o.reshape(M, D), w_o.astype(BF16), x2d)


def _ffn_layer(x2d, S, gain_row, w_up, dw_w, dw_b, w_down):
    d_ff = w_down.shape[0]
    tn = 512
    act = _proj(x2d, gain_row, [w_up.astype(BF16)], seq=S, epilogue="ffn", out_dtype=BF16,
                extras=(dw_w, dw_b.reshape(1, -1)), w_col_offsets=[0, d_ff // tn], tn=tn)
    return _matmul_residual(act, w_down.astype(BF16), x2d)


def kernel(x, mem, mem_norm, norm_mix, norm_cross, norm_ffn, final_norm, fox_w_in, fox_b_f, fox_q_gain, fox_k_gain, fox_w_o, rwkv_w_in, rwkv_mu, rwkv_w0, rwkv_decay_b, rwkv_a0, rwkv_iclr_b, rwkv_gate_b, rwkv_k_k, rwkv_k_a, rwkv_r_k, rwkv_ln_w, rwkv_ln_b, rwkv_w_o, conv_w_in, conv_dw_w, conv_dw_b, conv_ln_g, conv_ln_b, conv_w_o, cross_w_q, cross_w_kv, cross_w_o, ffn_w_up, ffn_dw_w, ffn_dw_b, ffn_w_down):
    B, S, D = x.shape
    depth = norm_mix.shape[0]
    Mem = mem.shape[1]
    M = B * S
    x2d = x.reshape(M, D)

    w_kv_all = jnp.concatenate([cross_w_kv[i] for i in range(depth)], axis=1).astype(BF16)
    kv_all = _proj(mem.reshape(B * Mem, D), mem_norm.reshape(1, D), [w_kv_all], seq=Mem,
                   epilogue="scale", out_dtype=BF16).reshape(B, Mem, depth * 2 * D)

    for i in range(depth):
        kind, j = i % 3, i // 3
        g_mix = norm_mix[i].reshape(1, D)
        if kind == 0:
            x2d = _fox_layer(x2d, B, S, g_mix, fox_w_in[j], fox_b_f[j], fox_q_gain[j], fox_k_gain[j],
                             fox_w_o[j])
        elif kind == 1:
            x2d = _rwkv_layer(x2d, B, S, g_mix, rwkv_w_in[j], rwkv_mu[j], rwkv_w0[j], rwkv_decay_b[j],
                              rwkv_a0[j], rwkv_iclr_b[j], rwkv_gate_b[j], rwkv_k_k[j], rwkv_k_a[j],
                              rwkv_r_k[j], rwkv_ln_w[j], rwkv_ln_b[j], rwkv_w_o[j])
        else:
            x2d = _conformer_layer(x2d, B, S, g_mix, conv_w_in[j], conv_dw_w[j], conv_dw_b[j],
                                   conv_ln_g[j], conv_ln_b[j], conv_w_o[j])
        x2d = _cross_layer(x2d, B, S, norm_cross[i].reshape(1, D), kv_all, i, cross_w_q[i], cross_w_o[i])
        x2d = _ffn_layer(x2d, S, norm_ffn[i].reshape(1, D), ffn_w_up[i], ffn_dw_w[i], ffn_dw_b[i],
                         ffn_w_down[i])
    return _rmsnorm(x2d, final_norm.reshape(1, D)).reshape(B, S, D)
```

```python
import functools

import jax
import jax.numpy as jnp
from jax import lax
from jax.experimental import pallas as pl
from jax.experimental.pallas import tpu as pltpu

F32 = jnp.float32
BF16 = jnp.bfloat16

RMS_EPS = 1e-6
LN_EPS = 1e-5
RWKV_GN_EPS = 64e-5

LANES = 128
HALO = 16
SUB_COLS = 256
VMEM_LIMIT = 56 * 1024 * 1024

FOX_HEADS = 16
FOX_HEAD_DIM = 128
MEM_HEADS = 4
RWKV_HEAD_DIM = 64
RWKV_CHUNK = 64
RWKV_GROUP_HEADS = 4
DECAY_LORA = 96
ICLR_LORA = 96
GATE_LORA = 256
CONV_HALO = 32
NEG_BIG = -1e30
LOG2E = 1.4426950408889634


def _cparams(*sem):
    return pltpu.CompilerParams(dimension_semantics=sem, vmem_limit_bytes=VMEM_LIMIT)


def _rms_rows(x, g):
    ms = jnp.mean(x * x, axis=-1, keepdims=True)
    return x * lax.rsqrt(ms + RMS_EPS) * g


def _sigmoid(x):
    return 1.0 / (1.0 + jnp.exp(-x))


def _dot(a, b):
    return jnp.dot(a, b, preferred_element_type=F32)


def _dot_nt(a, b):
    return lax.dot_general(a, b, (((1,), (1,)), ((), ())), preferred_element_type=F32)


def _dot_tn(a, b):
    return lax.dot_general(a, b, (((0,), (0,)), ((), ())), preferred_element_type=F32)


def _split3(x):
    hi = x.astype(BF16)
    r1 = x - hi.astype(F32)
    mid = r1.astype(BF16)
    lo = (r1 - mid.astype(F32)).astype(BF16)
    return hi, mid, lo


def _dot_exact_lhs(sel, x):
    hi, mid, lo = _split3(x)
    return _dot(sel, hi) + _dot(sel, mid) + _dot(sel, lo)


def _dot_exact_rhs(x, sel):
    hi, mid, lo = _split3(x)
    return _dot(hi, sel) + _dot(mid, sel) + _dot(lo, sel)


def _proj_body(*refs, n_w, halo, epilogue, deferred, tm, tn, tiles_per_seq, scale, n_row_tiles, n_col_tiles):
    it = iter(refs)
    x_ref = next(it)
    xp_ref = next(it) if halo else None
    g_ref = next(it)
    w_refs = [next(it) for _ in range(n_w)]
    if epilogue == "ffn":
        cw_refs = [next(it) for _ in range(n_w)]
        cb_refs = [next(it) for _ in range(n_w)]
    elif epilogue in ("lerp", "head_norm"):
        vec_ref = next(it)
    out_ref = next(it)
    h_scr = next(it)
    acc_scr = next(it)

    t = pl.program_id(0)
    total = n_row_tiles * n_col_tiles
    n = t % n_col_tiles
    m = jnp.minimum(t // n_col_tiles, n_row_tiles - 1)
    off = HALO if halo else 0

    if deferred:
        @pl.when(t == 0)
        def _():
            acc_scr[1] = jnp.zeros(acc_scr.shape[1:], F32)

    @pl.when((n == 0) & (t < total))
    def _():
        g = g_ref[...]
        h_scr[pl.ds(off, tm), :] = _rms_rows(x_ref[...], g).astype(BF16)
        if halo:
            hp = _rms_rows(xp_ref[...], g)
            first = (m % tiles_per_seq) == 0
            h_scr[pl.ds(0, HALO), :] = jnp.where(first, 0.0, hp).astype(BF16)

    sub = min(SUB_COLS, tn)
    col_slices = [slice(c * sub, (c + 1) * sub) for c in range(tn // sub)]

    def store(cs, val):
        out_ref[:, cs] = val.astype(out_ref.dtype)

    def epilogue_tile(cs, slot):
        def acc_rows(wi, cs, shift=0):
            return acc_scr[slot, wi, pl.ds(off - shift, tm), cs]

        if epilogue == "scale":
            store(cs, acc_rows(0, cs) * scale)
        elif epilogue == "sigmoid":
            store(cs, _sigmoid(acc_rows(0, cs)))
        elif epilogue == "glu":
            store(cs, acc_rows(0, cs) * _sigmoid(acc_rows(1, cs)))
        elif epilogue == "head_norm":
            for j in range(sub // FOX_HEAD_DIM):
                lo = cs.start + j * FOX_HEAD_DIM
                sl = slice(lo, lo + FOX_HEAD_DIM)
                store(sl, _rms_rows(acc_rows(0, sl), vec_ref[:, sl]))
        elif epilogue == "ffn":
            ys = []
            for wi, (cw_ref, cb_ref) in enumerate(zip(cw_refs, cb_refs)):
                cw = cw_ref[:, cs]
                taps = cw.shape[0]
                y = cb_ref[:, cs] + cw[taps - 1:taps, :] * acc_rows(wi, cs)
                for j in range(taps - 1):
                    y = y + cw[j:j + 1, :] * acc_rows(wi, cs, shift=taps - 1 - j)
                ys.append(y)
            u, gte = ys
            store(cs, gte * _sigmoid(gte) * u)
        elif epilogue == "lerp":
            cur = acc_rows(0, cs)
            store(cs, cur + (acc_rows(0, cs, shift=1) - cur) * vec_ref[:, cs])
        else:
            raise ValueError(epilogue)

    def matmul_tile(cs, slot):
        h = h_scr[...]
        for wi, w_ref in enumerate(w_refs):
            acc_scr[slot, wi, :, cs] = _dot(h, w_ref[:, cs])

    def deferred_step(slot):
        for cs in col_slices:
            epilogue_tile(cs, 1 - slot)
        for cs in col_slices:
            matmul_tile(cs, slot)

    if deferred:
        pl.when(t % 2 == 0)(functools.partial(deferred_step, 0))
        pl.when(t % 2 == 1)(functools.partial(deferred_step, 1))
    else:
        for cs in col_slices:
            matmul_tile(cs, 0)
            epilogue_tile(cs, 0)


def _proj(x2d, gain_row, ws, *, seq, epilogue, out_dtype, extras=(), scale=1.0, tm=512, tn=512,
          w_col_offsets=None, deferred=True):
    M, K = x2d.shape
    n_w = len(ws) if w_col_offsets is None else len(w_col_offsets)
    halo = epilogue in ("ffn", "lerp")
    tm = min(tm, M, seq)
    assert M % tm == 0 and seq % tm == 0 and tm % HALO == 0
    if w_col_offsets is None:
        w_col_offsets = [0] * n_w
        n_out = ws[0].shape[1]
        w_list = list(ws)
    else:
        w_list = [ws[0]] * n_w
        n_out = ws[0].shape[1] // n_w
    tn = min(tn, n_out)
    assert n_out % tn == 0
    MT, NT = M // tm, n_out // tn

    lag = 1 if deferred else 0
    cur_m = lambda t: jnp.minimum(t // NT, MT - 1)
    cur_n = lambda t: t % NT
    prev_m = lambda t: jnp.maximum(t - lag, 0) // NT
    prev_n = lambda t: jnp.maximum(t - lag, 0) % NT

    in_specs = [pl.BlockSpec((tm, K), lambda t: (cur_m(t), 0))]
    args = [x2d]
    if halo:
        r = tm // HALO
        in_specs.append(pl.BlockSpec((HALO, K), lambda t: (jnp.maximum(cur_m(t) * r - 1, 0), 0)))
        args.append(x2d)
    in_specs.append(pl.BlockSpec((1, K), lambda t: (0, 0)))
    args.append(gain_row)
    for w, o in zip(w_list, w_col_offsets):
        in_specs.append(pl.BlockSpec((K, tn), functools.partial(lambda t, o: (0, cur_n(t) + o), o=o)))
        args.append(w)
    if epilogue == "ffn":
        cw, cb = extras
        taps = cw.shape[0]
        assert taps - 1 <= HALO
        for o in w_col_offsets:
            in_specs.append(pl.BlockSpec((taps, tn), functools.partial(lambda t, o: (0, prev_n(t) + o), o=o)))
            args.append(cw)
        for o in w_col_offsets:
            in_specs.append(pl.BlockSpec((1, tn), functools.partial(lambda t, o: (0, prev_n(t) + o), o=o)))
            args.append(cb)
    elif epilogue in ("lerp", "head_norm"):
        in_specs.append(pl.BlockSpec((1, tn), lambda t: (0, prev_n(t))))
        args.append(extras[0])

    rows = tm + (HALO if halo else 0)
    scratch = [pltpu.VMEM((rows, K), BF16), pltpu.VMEM((1 + lag, n_w, rows, tn), F32)]

    body = functools.partial(_proj_body, n_w=n_w, halo=halo, epilogue=epilogue, deferred=deferred, tm=tm, tn=tn,
                             tiles_per_seq=seq // tm, scale=scale, n_row_tiles=MT, n_col_tiles=NT)
    return pl.pallas_call(
        body,
        grid=(MT * NT + lag,),
        in_specs=in_specs,
        out_specs=pl.BlockSpec((tm, tn), lambda t: (prev_m(t), prev_n(t))),
        out_shape=jax.ShapeDtypeStruct((M, n_out), out_dtype),
        scratch_shapes=scratch,
        compiler_params=_cparams("arbitrary"),
        name="proj_" + epilogue,
    )(*args)


def _mm_res_body(a_ref, w_ref, r_ref, o_ref):
    a = a_ref[...]
    sub = min(SUB_COLS, o_ref.shape[-1])
    for c in range(o_ref.shape[-1] // sub):
        cs = slice(c * sub, (c + 1) * sub)
        o_ref[:, cs] = r_ref[:, cs] + _dot(a, w_ref[:, cs])


def _matmul_residual(a, w, res, *, tm=None, tn=1024):
    M, K = a.shape
    N = w.shape[1]
    tm = min(tm or (1024 if K <= 2048 else 512), M)
    tn = min(tn, N)
    assert M % tm == 0 and N % tn == 0
    return pl.pallas_call(
        _mm_res_body,
        grid=(M // tm, N // tn),
        in_specs=[pl.BlockSpec((tm, K), lambda m, n: (m, 0)),
                  pl.BlockSpec((K, tn), lambda m, n: (0, n)),
                  pl.BlockSpec((tm, tn), lambda m, n: (m, n))],
        out_specs=pl.BlockSpec((tm, tn), lambda m, n: (m, n)),
        out_shape=jax.ShapeDtypeStruct((M, N), F32),
        compiler_params=_cparams("parallel", "parallel"),
        name="matmul_residual",
    )(a, w, res)


def _forget_body(x_ref, g_ref, w_ref, b_ref, c_ref, carry):
    s = pl.program_id(1)

    @pl.when(s == 0)
    def _():
        carry[...] = jnp.zeros_like(carry)

    h = _rms_rows(x_ref[...], g_ref[...]).astype(BF16)
    z = _dot(h, w_ref[...]) + b_ref[...]
    logf = jnp.minimum(z, 0.0) - jnp.log(1.0 + jnp.exp(-jnp.abs(z)))
    tm = logf.shape[0]
    row = lax.broadcasted_iota(jnp.int32, (tm, tm), 0)
    col = lax.broadcasted_iota(jnp.int32, (tm, tm), 1)
    tri = (col <= row).astype(BF16)
    c = _dot_exact_lhs(tri, logf) + carry[...]
    c_ref[...] = c * LOG2E
    carry[...] = c[tm - 1:tm, :]


def _forget_cumsum(x, gain_row, w_f, b_f, *, tm=256):
    B, S, K = x.shape
    tm = min(tm, S)
    return pl.pallas_call(
        _forget_body,
        grid=(B, S // tm),
        in_specs=[pl.BlockSpec((None, tm, K), lambda b, s: (b, s, 0)),
                  pl.BlockSpec((1, K), lambda b, s: (0, 0)),
                  pl.BlockSpec((K, LANES), lambda b, s: (0, 0)),
                  pl.BlockSpec((1, LANES), lambda b, s: (0, 0))],
        out_specs=pl.BlockSpec((None, tm, LANES), lambda b, s: (b, s, 0)),
        out_shape=jax.ShapeDtypeStruct((B, S, LANES), F32),
        scratch_shapes=[pltpu.VMEM((1, LANES), F32)],
        compiler_params=_cparams("parallel", "arbitrary"),
        name="fox_forget_cumsum",
    )(x, gain_row, w_f, b_f)


def _fox_attn_body(q_ref, k_ref, v_ref, gate_ref, ck_ref, o_ref, m_scr, l_scr, acc_scr, *, tq, tk, heads):
    i = pl.program_id(2)
    hd = FOX_HEAD_DIM
    rep = tk // LANES
    m_scr[...] = jnp.full_like(m_scr, NEG_BIG)
    l_scr[...] = jnp.zeros_like(l_scr)
    acc_scr[...] = jnp.zeros_like(acc_scr)

    def block(j, masked):
        start = pl.multiple_of(j * tk, tk)
        if masked:
            qpos = i * tq + lax.broadcasted_iota(jnp.int32, (tq, tk), 0)
            kpos = j * tk + lax.broadcasted_iota(jnp.int32, (tq, tk), 1)
            visible = kpos <= qpos
        for h in range(heads):
            sl = slice(h * hd, (h + 1) * hd)
            s = _dot_nt(q_ref[:, sl], k_ref[pl.ds(start, tk), sl]) - ck_ref[h, pl.ds(j, 1), :]
            if masked:
                s = jnp.where(visible, s, NEG_BIG)
            m_old = m_scr[h]
            m_new = jnp.maximum(m_old, jnp.max(s, axis=-1, keepdims=True))
            alpha = jnp.exp2(m_old - m_new)
            p = jnp.exp2(s - jnp.tile(m_new, (1, rep)))
            l_scr[h] = alpha * l_scr[h] + jnp.sum(p, axis=-1, keepdims=True)
            acc_scr[h] = alpha * acc_scr[h] + _dot(p.astype(BF16), v_ref[pl.ds(start, tk), sl])
            m_scr[h] = m_new

    n_full = (i * tq) // tk

    def pair_step(jj, c):
        block(2 * jj, False)
        block(2 * jj + 1, False)
        return c

    lax.fori_loop(0, n_full // 2, pair_step, 0)

    @pl.when(n_full % 2 == 1)
    def _():
        block(n_full - 1, False)

    block(n_full, True)
    for h in range(heads):
        sl = slice(h * hd, (h + 1) * hd)
        o = acc_scr[h] / l_scr[h]
        o_ref[:, sl] = (o * gate_ref[:, sl].astype(F32)).astype(o_ref.dtype)


def _fox_attention(qk, v, gate, ck_rows, *, tq=512, tk=512, heads=2):
    B, S, D = v.shape
    H = D // FOX_HEAD_DIM
    HG = H // heads
    W = heads * FOX_HEAD_DIM
    tk = min(tk, S)
    tq = min(tq, tk)
    assert tk % tq == 0 and S % tk == 0 and H % heads == 0
    body = functools.partial(_fox_attn_body, tq=tq, tk=tk, heads=heads)
    return pl.pallas_call(
        body,
        grid=(B, HG, S // tq),
        in_specs=[pl.BlockSpec((None, tq, W), lambda b, h, i: (b, i, h)),
                  pl.BlockSpec((None, S, W), lambda b, h, i: (b, 0, HG + h)),
                  pl.BlockSpec((None, S, W), lambda b, h, i: (b, 0, h)),
                  pl.BlockSpec((None, tq, W), lambda b, h, i: (b, i, h)),
                  pl.BlockSpec((None, heads, S // tk, tk), lambda b, h, i: (b, h, 0, 0))],
        out_specs=pl.BlockSpec((None, tq, W), lambda b, h, i: (b, i, h)),
        out_shape=jax.ShapeDtypeStruct((B, S, D), BF16),
        scratch_shapes=[pltpu.VMEM((heads, tq, LANES), F32), pltpu.VMEM((heads, tq, LANES), F32),
                        pltpu.VMEM((heads, tq, FOX_HEAD_DIM), F32)],
        compiler_params=_cparams("parallel", "parallel", "arbitrary"),
        name="fox_attention",
    )(qk, qk, v, gate, ck_rows)


def _cross_attn_body(q_ref, k_ref, v_ref, o_ref, *, heads):
    dh = q_ref.shape[-1] // heads
    for h in range(heads):
        sl = slice(h * dh, (h + 1) * dh)
        s = _dot_nt(q_ref[:, sl], k_ref[:, sl])
        s = s - jnp.max(s, axis=-1, keepdims=True)
        p = jnp.exp(s)
        l = jnp.sum(p, axis=-1, keepdims=True)
        o = _dot(p.astype(BF16), v_ref[:, sl])
        o_ref[:, sl] = (o / l).astype(o_ref.dtype)


def _cross_attention(q, kv_all, layer, *, tq=512):
    B, S, D = q.shape
    Mem = kv_all.shape[1]
    tq = min(tq, S)
    body = functools.partial(_cross_attn_body, heads=MEM_HEADS)
    return pl.pallas_call(
        body,
        grid=(B, S // tq),
        in_specs=[pl.BlockSpec((None, tq, D), lambda b, i: (b, i, 0)),
                  pl.BlockSpec((None, Mem, D), lambda b, i: (b, 0, 2 * layer)),
                  pl.BlockSpec((None, Mem, D), lambda b, i: (b, 0, 2 * layer + 1))],
        out_specs=pl.BlockSpec((None, tq, D), lambda b, i: (b, i, 0)),
        out_shape=jax.ShapeDtypeStruct((B, S, D), BF16),
        compiler_params=_cparams("parallel", "parallel"),
        name="cross_attention",
    )(q, kv_all, kv_all)


def _conformer_body(z_ref, zp_ref, w_ref, b_ref, g_ref, beta_ref, o_ref, zs, ys, *, tm, width):
    i = pl.program_id(1)
    zs[pl.ds(CONV_HALO, tm), :] = z_ref[...]
    zs[pl.ds(0, CONV_HALO), :] = jnp.where(i == 0, 0.0, zp_ref[...])
    D = z_ref.shape[-1]
    base = CONV_HALO - (width - 1)
    for c in range(D // LANES):
        sl = slice(c * LANES, (c + 1) * LANES)
        acc = jnp.broadcast_to(b_ref[:, sl], (tm, LANES))
        for k in range(width):
            acc = acc + w_ref[k:k + 1, sl] * zs[pl.ds(base + k, tm), sl]
        ys[:, sl] = acc
    y = ys[...]
    mu = jnp.mean(y, axis=-1, keepdims=True)
    d = y - mu
    var = jnp.mean(d * d, axis=-1, keepdims=True)
    t = d * lax.rsqrt(var + LN_EPS) * g_ref[...] + beta_ref[...]
    o_ref[...] = (t * _sigmoid(t)).astype(o_ref.dtype)


def _conformer_conv(z, dw_w, dw_b, ln_g, ln_b, *, tm=256):
    B, S, D = z.shape
    width = dw_w.shape[0]
    tm = min(tm, S)
    assert width - 1 <= CONV_HALO and tm % CONV_HALO == 0 and S % tm == 0
    r = tm // CONV_HALO
    body = functools.partial(_conformer_body, tm=tm, width=width)
    row = lambda a: a.reshape(1, D)
    return pl.pallas_call(
        body,
        grid=(B, S // tm),
        in_specs=[pl.BlockSpec((None, tm, D), lambda b, i: (b, i, 0)),
                  pl.BlockSpec((None, CONV_HALO, D), lambda b, i: (b, jnp.maximum(i * r - 1, 0), 0)),
                  pl.BlockSpec((width, D), lambda b, i: (0, 0)),
                  pl.BlockSpec((1, D), lambda b, i: (0, 0)),
                  pl.BlockSpec((1, D), lambda b, i: (0, 0)),
                  pl.BlockSpec((1, D), lambda b, i: (0, 0))],
        out_specs=pl.BlockSpec((None, tm, D), lambda b, i: (b, i, 0)),
        out_shape=jax.ShapeDtypeStruct((B, S, D), BF16),
        scratch_shapes=[pltpu.VMEM((tm + CONV_HALO, D), F32), pltpu.VMEM((tm, D), F32)],
        compiler_params=_cparams("parallel", "parallel"),
        name="conformer_conv_ln",
    )(z, z, dw_w, row(dw_b), row(ln_g), row(ln_b))


def _head_sum(x, head_dim):
    n = x.shape[-1]
    r = lax.broadcasted_iota(jnp.int32, (LANES, LANES), 0) // head_dim
    c = lax.broadcasted_iota(jnp.int32, (LANES, LANES), 1) // head_dim
    ones = (r == c).astype(BF16)
    parts = []
    for j in range(n // LANES):
        parts.append(_dot_exact_rhs(x[:, j * LANES:(j + 1) * LANES], ones))
    return parts[0] if len(parts) == 1 else jnp.concatenate(parts, axis=-1)


def _rwkv_prep_body(p_ref, w0_ref, db_ref, a0_ref, ib_ref, gb_ref, kk_ref, ka_ref, rk_ref,
                    r_o, k_o, v_o, kk_o, b_o, lw_o, bonus_o, g_o, *, D):
    lw_pad = LANES
    r = p_ref[:, 0:D]
    k = p_ref[:, D:2 * D]
    v = p_ref[:, 2 * D:3 * D]
    o = 3 * D
    w_lo = p_ref[:, o:o + lw_pad]
    a_lo = p_ref[:, o + lw_pad:o + 2 * lw_pad]
    g_lo = p_ref[:, o + 2 * lw_pad:o + 2 * lw_pad + GATE_LORA]

    w_arg = w0_ref[...] + _dot(jnp.tanh(w_lo).astype(BF16), db_ref[...])
    softplus = jnp.maximum(-w_arg, 0.0) + jnp.log(1.0 + jnp.exp(-jnp.abs(w_arg)))
    lw_o[...] = -jnp.exp(-softplus - 0.5)
    a = _sigmoid(a0_ref[...] + _dot(a_lo.astype(BF16), ib_ref[...]))
    g_o[...] = _dot(_sigmoid(g_lo).astype(BF16), gb_ref[...])
    kk = k * kk_ref[...]
    nrm = jnp.sqrt(_head_sum(kk * kk, RWKV_HEAD_DIM))
    kk = kk / jnp.maximum(nrm, 1e-12)
    k2 = k * (1.0 + (a - 1.0) * ka_ref[...])
    r_o[...] = r
    k_o[...] = k2
    v_o[...] = v
    kk_o[...] = kk
    b_o[...] = kk * a
    bonus_o[...] = _head_sum(r * k2 * rk_ref[...], RWKV_HEAD_DIM) * v


def _rwkv_prep(proj, w0, decay_b, a0, iclr_b, gate_b, k_k, k_a, r_k, *, D, tm=256):
    M, P = proj.shape
    tm = min(tm, M)
    body = functools.partial(_rwkv_prep_body, D=D)
    row = lambda a: a.reshape(1, D)
    full = lambda a: pl.BlockSpec(a.shape, lambda m: (0, 0))
    ins = [proj, row(w0), decay_b, row(a0), iclr_b, gate_b, row(k_k), row(k_a), row(r_k)]
    out_spec = pl.BlockSpec((tm, D), lambda m: (m, 0))
    return pl.pallas_call(
        body,
        grid=(M // tm,),
        in_specs=[pl.BlockSpec((tm, P), lambda m: (m, 0))] + [full(a) for a in ins[1:]],
        out_specs=[out_spec] * 8,
        out_shape=[jax.ShapeDtypeStruct((M, D), F32)] * 8,
        compiler_params=_cparams("parallel"),
        name="rwkv_prep",
    )(*ins)


def _rwkv_scan_body(r_ref, k_ref, v_ref, kk_ref, b_ref, lw_ref, bonus_ref, g_ref, lnw_ref, lnb_ref,
                    o_ref, state, *, C, D):
    c_idx = pl.program_id(0)

    @pl.when(c_idx == 0)
    def _():
        state[...] = jnp.zeros_like(state)

    N = RWKV_HEAD_DIM
    GH = RWKV_GROUP_HEADS
    GW = GH * N
    n_groups = D // GW
    n_batch = r_ref.shape[0]
    bf = lambda a: a.astype(BF16)

    row = lax.broadcasted_iota(jnp.int32, (C, GW), 0)
    lane = lax.broadcasted_iota(jnp.int32, (C, GW), 1)
    strict = (lane % N) < row
    incl = (lane % N) <= row
    eye = ((lane % N) == row).astype(F32)
    head_of_lane = lax.broadcasted_iota(jnp.int32, (1, GW), 1) // N
    bd_mask = (lax.broadcasted_iota(jnp.int32, (GW, GW), 0) // N) == (lax.broadcasted_iota(jnp.int32, (GW, GW), 1) // N)
    tri = (lax.broadcasted_iota(jnp.int32, (C, C), 1) <= lax.broadcasted_iota(jnp.int32, (C, C), 0)).astype(BF16)

    def block_diag(x):
        return jnp.concatenate([jnp.where(head_of_lane == h, x, jnp.zeros_like(x)) for h in range(GH)], axis=0)

    units = [(bi, g) for bi in range(n_batch) for g in range(n_groups)]
    gsl = lambda g: slice(g * GW, (g + 1) * GW)

    pre = []
    for bi in range(n_batch):
        lw = lw_ref[bi]
        cum = _dot_exact_lhs(tri, lw)
        cum_last = cum[C - 1:C, :]
        g_inv = jnp.exp(-cum)
        g_tail = jnp.exp(cum_last - cum)
        kf = k_ref[bi]
        bfl = b_ref[bi]
        pre.append(dict(
            r_t=bf(r_ref[bi] * jnp.exp(cum)), kk_t=bf(kk_ref[bi] * jnp.exp(cum - lw)),
            k_t=bf(kf * g_inv), b_t=bf(bfl * g_inv), k_h=bf(kf * g_tail), b_h=bf(bfl * g_tail),
            v=bf(v_ref[bi]), g_last=jnp.exp(cum_last)))

    lhs_kr, Ls, Akks, RKs, Vbd = [], [], [], [], []
    for bi, g in units:
        p = pre[bi]
        lhs = jnp.concatenate([p["kk_t"][:, gsl(g)], p["r_t"][:, gsl(g)]], axis=0)
        ab = _dot_nt(lhs, block_diag(p["b_t"][:, gsl(g)]))
        ak = _dot_nt(lhs, block_diag(p["k_t"][:, gsl(g)]))
        lhs_kr.append(lhs)
        Ls.append(jnp.where(strict, ab[0:C], 0.0))
        Akks.append(bf(jnp.where(strict, ak[0:C], 0.0)))
        RKs.append(jnp.concatenate([bf(jnp.where(incl, ab[C:2 * C], 0.0)),
                                    bf(jnp.where(incl, ak[C:2 * C], 0.0))], axis=1))
        Vbd.append(block_diag(p["v"][:, gsl(g)]))
    AkkVs = [_dot(a, vb) for a, vb in zip(Akks, Vbd)]

    Xs = [eye - L for L in Ls]
    Qs = [_dot(bf(L), block_diag(bf(L))) for L in Ls]
    span = 4
    while span < C:
        nxt = [_dot(jnp.concatenate([bf(Q), bf(X)], axis=0), block_diag(bf(Q))) for Q, X in zip(Qs, Xs)]
        Qs = [n[0:C] for n in nxt]
        Xs = [X + n[C:2 * C] for X, n in zip(Xs, nxt)]
        span *= 2
    Xs = [X + _dot(bf(X), block_diag(bf(Q))) for Q, X in zip(Qs, Xs)]

    KRS = [_dot_nt(lhs, bf(state[bi, g])) for lhs, (bi, g) in zip(lhs_kr, units)]
    Es = [-_dot(bf(X), block_diag(bf(krs[0:C] + akkv))) for X, krs, akkv in zip(Xs, KRS, AkkVs)]
    ys = [[None] * n_groups for _ in range(n_batch)]
    for i, (bi, g) in enumerate(units):
        p = pre[bi]
        e16 = bf(Es[i])
        ys[bi][g] = KRS[i][C:2 * C] + _dot(RKs[i], jnp.concatenate([block_diag(e16), Vbd[i]], axis=0))
        upd = _dot_tn(jnp.concatenate([e16, p["v"][:, gsl(g)]], axis=0),
                      jnp.concatenate([p["b_h"][:, gsl(g)], p["k_h"][:, gsl(g)]], axis=0))
        state[bi, g] = state[bi, g] * p["g_last"][:, gsl(g)] + jnp.where(bd_mask, upd, 0.0)

    for bi in range(n_batch):
        y = jnp.concatenate(ys[bi], axis=-1)
        mean = _head_sum(y, N) * (1.0 / N)
        d = y - mean
        var = _head_sum(d * d, N) * (1.0 / N)
        yn = d * lax.rsqrt(var + RWKV_GN_EPS) * lnw_ref[...] + lnb_ref[...] + bonus_ref[bi]
        o_ref[bi] = (yn * g_ref[bi]).astype(o_ref.dtype)


def _rwkv_scan(r, k, v, kk, b, lw, bonus, g, ln_w, ln_b, *, B, S):
    M, D = r.shape
    C = RWKV_CHUNK
    assert S % C == 0 and C == RWKV_HEAD_DIM and D % (RWKV_GROUP_HEADS * RWKV_HEAD_DIM) == 0
    nc = S // C
    GW = RWKV_GROUP_HEADS * RWKV_HEAD_DIM
    body = functools.partial(_rwkv_scan_body, C=C, D=D)
    tile = pl.BlockSpec((B, C, D), lambda c: (0, c, 0))
    vec = pl.BlockSpec((1, D), lambda c: (0, 0))
    seq = lambda a: a.reshape(B, S, D)
    return pl.pallas_call(
        body,
        grid=(nc,),
        in_specs=[tile] * 8 + [vec, vec],
        out_specs=tile,
        out_shape=jax.ShapeDtypeStruct((B, S, D), BF16),
        scratch_shapes=[pltpu.VMEM((B, D // GW, GW, GW), F32)],
        compiler_params=_cparams("arbitrary"),
        name="rwkv_scan",
    )(*(seq(a) for a in (r, k, v, kk, b, lw, bonus, g)), ln_w.reshape(1, D), ln_b.reshape(1, D)).reshape(M, D)


def _rmsnorm_body(x_ref, g_ref, o_ref):
    o_ref[...] = _rms_rows(x_ref[...], g_ref[...])


def _rmsnorm(x2d, gain_row, *, tm=512):
    M, K = x2d.shape
    tm = min(tm, M)
    return pl.pallas_call(
        _rmsnorm_body,
        grid=(M // tm,),
        in_specs=[pl.BlockSpec((tm, K), lambda m: (m, 0)), pl.BlockSpec((1, K), lambda m: (0, 0))],
        out_specs=pl.BlockSpec((tm, K), lambda m: (m, 0)),
        out_shape=jax.ShapeDtypeStruct((M, K), F32),
        compiler_params=_cparams("parallel"),
        name="final_rmsnorm",
    )(x2d, gain_row)


def _pad_cols(a, n):
    return jnp.pad(a, ((0, 0), (0, n - a.shape[1])))


def _pad_rows(a, n):
    return jnp.pad(a, ((0, n - a.shape[0]), (0, 0)))


def _fox_layer(x2d, B, S, gain_row, w_in, b_f, q_gain, k_gain, w_o, *, tm, attn_cfg):
    M, D = x2d.shape
    H = FOX_HEADS
    w_f = _pad_cols(w_in[:, 4 * D:], LANES).astype(BF16)
    b_row = _pad_cols(b_f.reshape(1, H), LANES)
    scale = FOX_HEAD_DIM ** -0.5
    gains = jnp.concatenate([jnp.tile(q_gain * (scale * LOG2E), H), jnp.tile(k_gain, H)]).reshape(1, 2 * D)
    proj = functools.partial(_proj, x2d, gain_row, seq=S, out_dtype=BF16, tn=1024, tm=tm)
    qk = proj([w_in[:, :2 * D].astype(BF16)], epilogue="head_norm", extras=(gains,))
    v = proj([w_in[:, 2 * D:3 * D].astype(BF16)], epilogue="scale")
    gate = proj([w_in[:, 3 * D:4 * D].astype(BF16)], epilogue="sigmoid")
    c = _forget_cumsum(x2d.reshape(B, S, D), gain_row, w_f, b_row)
    tk = min(attn_cfg["tk"], S)
    ck_rows = c[:, :, :H].transpose(0, 2, 1).reshape(B, H, S // tk, tk)
    o = _fox_attention(qk.reshape(B, S, 2 * D), v.reshape(B, S, D), gate.reshape(B, S, D), ck_rows,
                       **dict(attn_cfg, tk=tk))
    return _matmul_residual(o.reshape(M, D), w_o.astype(BF16), x2d)


def _rwkv_layer(x2d, B, S, gain_row, w_in, mu, w0, decay_b, a0, iclr_b, gate_b, k_k, k_a, r_k,
                ln_w, ln_b, w_o):
    M, D = x2d.shape
    o3 = 3 * D
    segs = [(0, o3, o3), (o3, DECAY_LORA, LANES), (o3 + DECAY_LORA, ICLR_LORA, LANES),
            (o3 + DECAY_LORA + ICLR_LORA, GATE_LORA, GATE_LORA)]
    w_pad = jnp.concatenate([_pad_cols(w_in[:, s:s + n], p) for s, n, p in segs], axis=1)
    mu_pad = jnp.concatenate([_pad_cols(mu[None, s:s + n], p) for s, n, p in segs], axis=1)
    P = w_pad.shape[1]
    tn = 512
    Pp = -(-P // tn) * tn
    w_pad = _pad_cols(w_pad, Pp).astype(BF16)
    mu_pad = _pad_cols(mu_pad, Pp)
    proj = _proj(x2d, gain_row, [w_pad], seq=S, epilogue="lerp", out_dtype=F32, extras=(mu_pad,), tn=tn,
                 tm=1024, deferred=False)
    r, k, v, kk, b, lw, bonus, g = _rwkv_prep(
        proj, w0, _pad_rows(decay_b, LANES).astype(BF16), a0, _pad_rows(iclr_b, LANES).astype(BF16),
        gate_b.astype(BF16), k_k, k_a, r_k.reshape(-1), D=D)
    y = _rwkv_scan(r, k, v, kk, b, lw, bonus, g, ln_w, ln_b, B=B, S=S)
    return _matmul_residual(y, w_o.astype(BF16), x2d)


def _conformer_layer(x2d, B, S, gain_row, w_in, dw_w, dw_b, ln_g, ln_b, w_o):
    M, D = x2d.shape
    tn = 512
    z = _proj(x2d, gain_row, [w_in.astype(BF16)], seq=S, epilogue="glu", out_dtype=F32,
              w_col_offsets=[0, D // tn], tn=tn, tm=1024)
    t = _conformer_conv(z.reshape(B, S, D), dw_w, dw_b, ln_g, ln_b)
    return _matmul_residual(t.reshape(M, D), w_o.astype(BF16), x2d)


def _cross_layer(x2d, B, S, gain_row, kv_all, layer, w_q, w_o, *, q_cfg, o_cfg):
    M, D = x2d.shape
    scale = (D // MEM_HEADS) ** -0.5
    q = _proj(x2d, gain_row, [w_q.astype(BF16)], seq=S, epilogue="scale", out_dtype=BF16, scale=scale, **q_cfg)
    o = _cross_attention(q.reshape(B, S, D), kv_all, layer)
    return _matmul_residual(o.reshape(M, D), w_o.astype(BF16), x2d, **o_cfg)


def _ffn_layer(x2d, S, gain_row, w_up, dw_w, dw_b, w_down, *, up_cfg, down_cfg):
    d_ff = w_down.shape[0]
    tn = 512
    act = _proj(x2d, gain_row, [w_up.astype(BF16)], seq=S, epilogue="ffn", out_dtype=BF16,
                extras=(dw_w, dw_b.reshape(1, -1)), w_col_offsets=[0, d_ff // tn], tn=tn, **up_cfg)
    return _matmul_residual(act, w_down.astype(BF16), x2d, **down_cfg)


def kernel(x, mem, mem_norm, norm_mix, norm_cross, norm_ffn, final_norm, fox_w_in, fox_b_f, fox_q_gain, fox_k_gain, fox_w_o, rwkv_w_in, rwkv_mu, rwkv_w0, rwkv_decay_b, rwkv_a0, rwkv_iclr_b, rwkv_gate_b, rwkv_k_k, rwkv_k_a, rwkv_r_k, rwkv_ln_w, rwkv_ln_b, rwkv_w_o, conv_w_in, conv_dw_w, conv_dw_b, conv_ln_g, conv_ln_b, conv_w_o, cross_w_q, cross_w_kv, cross_w_o, ffn_w_up, ffn_dw_w, ffn_dw_b, ffn_w_down):
    B, S, D = x.shape
    depth = norm_mix.shape[0]
    Mem = mem.shape[1]
    M = B * S
    x2d = x.reshape(M, D)

    w_kv_all = jnp.concatenate([cross_w_kv[i] for i in range(depth)], axis=1).astype(BF16)
    kv_all = _proj(mem.reshape(B * Mem, D), mem_norm.reshape(1, D), [w_kv_all], seq=Mem,
                   epilogue="scale", out_dtype=BF16).reshape(B, Mem, depth * 2 * D)

    for i in range(depth):
        kind, j = i % 3, i // 3
        g_mix = norm_mix[i].reshape(1, D)
        if kind == 0:
            x2d = _fox_layer(x2d, B, S, g_mix, fox_w_in[j], fox_b_f[j], fox_q_gain[j], fox_k_gain[j],
                             fox_w_o[j], tm=[512, 1024][j],
                             attn_cfg=[dict(tq=512, tk=512, heads=2), dict(tq=512, tk=1024, heads=2)][j])
        elif kind == 1:
            x2d = _rwkv_layer(x2d, B, S, g_mix, rwkv_w_in[j], rwkv_mu[j], rwkv_w0[j], rwkv_decay_b[j],
                              rwkv_a0[j], rwkv_iclr_b[j], rwkv_gate_b[j], rwkv_k_k[j], rwkv_k_a[j],
                              rwkv_r_k[j], rwkv_ln_w[j], rwkv_ln_b[j], rwkv_w_o[j])
        else:
            x2d = _conformer_layer(x2d, B, S, g_mix, conv_w_in[j], conv_dw_w[j], conv_dw_b[j],
                                   conv_ln_g[j], conv_ln_b[j], conv_w_o[j])
        q_cfg = [dict(tm=512, tn=1024, deferred=True), dict(tm=1024, tn=1024, deferred=True),
                 dict(tm=1024, tn=1024, deferred=False), dict(tm=512, tn=512, deferred=False)][i]
        o_cfg = [dict(tm=1024, tn=1024), dict(tm=1024, tn=1024), dict(tm=512, tn=2048), dict(tm=512, tn=2048)][i]
        up_cfg = [dict(tm=512, deferred=False), dict(tm=1024, deferred=False),
                  dict(tm=1024, deferred=True), dict(tm=512, deferred=True)][i]
        down_cfg = [dict(tm=512, tn=1024), dict(tm=512, tn=1024), dict(tm=1024, tn=512), dict(tm=1024, tn=512)][i]
        x2d = _cross_layer(x2d, B, S, norm_cross[i].reshape(1, D), kv_all, i, cross_w_q[i], cross_w_o[i],
                           q_cfg=q_cfg, o_cfg=o_cfg)
        x2d = _ffn_layer(x2d, S, norm_ffn[i].reshape(1, D), ffn_w_up[i], ffn_dw_w[i], ffn_dw_b[i],
                         ffn_w_down[i], up_cfg=up_cfg, down_cfg=down_cfg)
    return _rmsnorm(x2d, final_norm.reshape(1, D)).reshape(B, S, D)
```

```python
import functools

import jax
import jax.numpy as jnp
from jax import lax
from jax.experimental import pallas as pl
from jax.experimental.pallas import tpu as pltpu

F32 = jnp.float32
BF16 = jnp.bfloat16

RMS_EPS = 1e-6
LN_EPS = 1e-5
RWKV_GN_EPS = 64e-5

LANES = 128
HALO = 16
SUB_COLS = 256
VMEM_LIMIT = 56 * 1024 * 1024

FOX_HEADS = 16
FOX_HEAD_DIM = 128
FOX_KEY_BLOCK = 1024
MEM_HEADS = 4
RWKV_HEAD_DIM = 64
RWKV_CHUNK = 64
RWKV_GROUP_HEADS = 4
DECAY_LORA = 96
ICLR_LORA = 96
GATE_LORA = 256
CONV_HALO = 32
NEG_BIG = -1e30
LOG2E = 1.4426950408889634


def _cparams(*sem):
    return pltpu.CompilerParams(dimension_semantics=sem, vmem_limit_bytes=VMEM_LIMIT)


def _rms_rows(x, g):
    ms = jnp.mean(x * x, axis=-1, keepdims=True)
    return x * lax.rsqrt(ms + RMS_EPS) * g


def _sigmoid(x):
    return 0.5 * jnp.tanh(0.5 * x) + 0.5


def _dot(a, b):
    return jnp.dot(a, b, preferred_element_type=F32)


def _dot_nt(a, b):
    return lax.dot_general(a, b, (((1,), (1,)), ((), ())), preferred_element_type=F32)


def _dot_tn(a, b):
    return lax.dot_general(a, b, (((0,), (0,)), ((), ())), preferred_element_type=F32)


def _split3(x):
    hi = x.astype(BF16)
    r1 = x - hi.astype(F32)
    mid = r1.astype(BF16)
    lo = (r1 - mid.astype(F32)).astype(BF16)
    return hi, mid, lo


def _dot_exact_lhs(sel, x):
    hi, mid, lo = _split3(x)
    return _dot(sel, hi) + _dot(sel, mid) + _dot(sel, lo)


def _dot_exact_rhs(x, sel):
    hi, mid, lo = _split3(x)
    return _dot(hi, sel) + _dot(mid, sel) + _dot(lo, sel)


def _proj_body(*refs, n_w, halo, epilogue, tm, tn, tiles_per_seq, scale):
    it = iter(refs)
    x_ref = next(it)
    xp_ref = next(it) if halo else None
    g_ref = next(it)
    w_refs = [next(it) for _ in range(n_w)]
    if epilogue == "ffn":
        cw_refs = [next(it) for _ in range(n_w)]
        cb_refs = [next(it) for _ in range(n_w)]
    elif epilogue in ("lerp", "head_norm"):
        vec_ref = next(it)
    out_ref = next(it)
    h_scr = next(it)
    acc_scr = next(it)

    m = pl.program_id(0)
    n = pl.program_id(1)
    off = HALO if halo else 0

    @pl.when(n == 0)
    def _():
        g = g_ref[...]
        h_scr[pl.ds(off, tm), :] = _rms_rows(x_ref[...], g).astype(BF16)
        if halo:
            hp = _rms_rows(xp_ref[...], g)
            first = (m % tiles_per_seq) == 0
            h_scr[pl.ds(0, HALO), :] = jnp.where(first, 0.0, hp).astype(BF16)

    sub = min(SUB_COLS, tn)
    col_slices = [slice(c * sub, (c + 1) * sub) for c in range(tn // sub)]

    def store(cs, val):
        out_ref[:, cs] = val.astype(out_ref.dtype)

    def acc_rows(wi, cs, shift=0):
        return acc_scr[wi, pl.ds(off - shift, tm), cs]

    def epilogue_tile(cs):
        if epilogue == "scale":
            store(cs, acc_rows(0, cs) * scale)
        elif epilogue == "sigmoid":
            store(cs, _sigmoid(acc_rows(0, cs)))
        elif epilogue == "glu":
            store(cs, acc_rows(0, cs) * _sigmoid(acc_rows(1, cs)))
        elif epilogue == "head_norm":
            for j in range(sub // FOX_HEAD_DIM):
                lo = cs.start + j * FOX_HEAD_DIM
                sl = slice(lo, lo + FOX_HEAD_DIM)
                store(sl, _rms_rows(acc_rows(0, sl), vec_ref[:, sl]))
        elif epilogue == "ffn":
            ys = []
            for wi, (cw_ref, cb_ref) in enumerate(zip(cw_refs, cb_refs)):
                cw = cw_ref[:, cs]
                taps = cw.shape[0]
                y = cb_ref[:, cs] + cw[taps - 1:taps, :] * acc_rows(wi, cs)
                for j in range(taps - 1):
                    y = y + cw[j:j + 1, :] * acc_rows(wi, cs, shift=taps - 1 - j)
                ys.append(y)
            u, gte = ys
            store(cs, gte * _sigmoid(gte) * u)
        elif epilogue == "lerp":
            cur = acc_rows(0, cs)
            store(cs, cur + (acc_rows(0, cs, shift=1) - cur) * vec_ref[:, cs])
        else:
            raise ValueError(epilogue)

    h = h_scr[...]
    for cs in col_slices:
        for wi, w_ref in enumerate(w_refs):
            acc_scr[wi, :, cs] = _dot(h, w_ref[:, cs])
        epilogue_tile(cs)


def _proj(x2d, gain_row, ws, *, seq, epilogue, out_dtype, extras=(), scale=1.0, tm=1024, tn=512,
          w_col_offsets=None):
    M, K = x2d.shape
    n_w = len(ws) if w_col_offsets is None else len(w_col_offsets)
    halo = epilogue in ("ffn", "lerp")
    tm = min(tm, M, seq)
    assert M % tm == 0 and seq % tm == 0 and tm % HALO == 0
    if w_col_offsets is None:
        w_col_offsets = [0] * n_w
        n_out = ws[0].shape[1]
        w_list = list(ws)
    else:
        w_list = [ws[0]] * n_w
        n_out = ws[0].shape[1] // n_w
    tn = min(tn, n_out)
    assert n_out % tn == 0

    def cols(o):
        return lambda m, n: (0, n + o)

    in_specs = [pl.BlockSpec((tm, K), lambda m, n: (m, 0))]
    args = [x2d]
    if halo:
        r = tm // HALO
        in_specs.append(pl.BlockSpec((HALO, K), lambda m, n: (jnp.maximum(m * r - 1, 0), 0)))
        args.append(x2d)
    in_specs.append(pl.BlockSpec((1, K), lambda m, n: (0, 0)))
    args.append(gain_row)
    for w, o in zip(w_list, w_col_offsets):
        in_specs.append(pl.BlockSpec((K, tn), cols(o)))
        args.append(w)
    if epilogue == "ffn":
        cw, cb = extras
        taps = cw.shape[0]
        assert taps - 1 <= HALO
        for o in w_col_offsets:
            in_specs.append(pl.BlockSpec((taps, tn), cols(o)))
            args.append(cw)
        for o in w_col_offsets:
            in_specs.append(pl.BlockSpec((1, tn), cols(o)))
            args.append(cb)
    elif epilogue in ("lerp", "head_norm"):
        in_specs.append(pl.BlockSpec((1, tn), cols(0)))
        args.append(extras[0])

    rows = tm + (HALO if halo else 0)
    scratch = [pltpu.VMEM((rows, K), BF16), pltpu.VMEM((n_w, rows, tn), F32)]

    body = functools.partial(_proj_body, n_w=n_w, halo=halo, epilogue=epilogue, tm=tm, tn=tn,
                             tiles_per_seq=seq // tm, scale=scale)
    return pl.pallas_call(
        body,
        grid=(M // tm, n_out // tn),
        in_specs=in_specs,
        out_specs=pl.BlockSpec((tm, tn), lambda m, n: (m, n)),
        out_shape=jax.ShapeDtypeStruct((M, n_out), out_dtype),
        scratch_shapes=scratch,
        compiler_params=_cparams("parallel", "arbitrary"),
        name="proj_" + epilogue,
    )(*args)


def _mm_res_body(a_ref, w_ref, r_ref, o_ref):
    a = a_ref[...]
    sub = min(SUB_COLS, o_ref.shape[-1])
    for c in range(o_ref.shape[-1] // sub):
        cs = slice(c * sub, (c + 1) * sub)
        o_ref[:, cs] = r_ref[:, cs] + _dot(a, w_ref[:, cs])


def _matmul_residual(a, w, res):
    M, K = a.shape
    N = w.shape[1]
    tm, tn = (512, 2048) if K <= 2048 else (1024, 512)
    tm, tn = min(tm, M), min(tn, N)
    assert M % tm == 0 and N % tn == 0
    return pl.pallas_call(
        _mm_res_body,
        grid=(M // tm, N // tn),
        in_specs=[pl.BlockSpec((tm, K), lambda m, n: (m, 0)),
                  pl.BlockSpec((K, tn), lambda m, n: (0, n)),
                  pl.BlockSpec((tm, tn), lambda m, n: (m, n))],
        out_specs=pl.BlockSpec((tm, tn), lambda m, n: (m, n)),
        out_shape=jax.ShapeDtypeStruct((M, N), F32),
        compiler_params=_cparams("parallel", "parallel"),
        name="matmul_residual",
    )(a, w, res)


def _forget_body(x_ref, g_ref, w_ref, b_ref, c_ref, carry):
    s = pl.program_id(1)

    @pl.when(s == 0)
    def _():
        carry[...] = jnp.zeros_like(carry)

    h = _rms_rows(x_ref[...], g_ref[...]).astype(BF16)
    z = _dot(h, w_ref[...]) + b_ref[...]
    logf = jnp.minimum(z, 0.0) - jnp.log(1.0 + jnp.exp(-jnp.abs(z)))
    tm = logf.shape[0]
    row = lax.broadcasted_iota(jnp.int32, (tm, tm), 0)
    col = lax.broadcasted_iota(jnp.int32, (tm, tm), 1)
    tri = (col <= row).astype(BF16)
    c = _dot_exact_lhs(tri, logf) + carry[...]
    c_ref[...] = c * LOG2E
    carry[...] = c[tm - 1:tm, :]


def _forget_cumsum(x, gain_row, w_f, b_f, *, tm=256):
    B, S, K = x.shape
    tm = min(tm, S)
    return pl.pallas_call(
        _forget_body,
        grid=(B, S // tm),
        in_specs=[pl.BlockSpec((None, tm, K), lambda b, s: (b, s, 0)),
                  pl.BlockSpec((1, K), lambda b, s: (0, 0)),
                  pl.BlockSpec((K, LANES), lambda b, s: (0, 0)),
                  pl.BlockSpec((1, LANES), lambda b, s: (0, 0))],
        out_specs=pl.BlockSpec((None, tm, LANES), lambda b, s: (b, s, 0)),
        out_shape=jax.ShapeDtypeStruct((B, S, LANES), F32),
        scratch_shapes=[pltpu.VMEM((1, LANES), F32)],
        compiler_params=_cparams("parallel", "arbitrary"),
        name="fox_forget_cumsum",
    )(x, gain_row, w_f, b_f)


def _fox_attn_body(q_ref, k_ref, v_ref, gate_ref, ck_ref, o_ref, m_scr, l_scr, acc_scr, *, tq, tk, heads):
    i = pl.program_id(2)
    hd = FOX_HEAD_DIM
    rep = tk // LANES
    m_scr[...] = jnp.full_like(m_scr, NEG_BIG)
    l_scr[...] = jnp.zeros_like(l_scr)
    acc_scr[...] = jnp.zeros_like(acc_scr)

    def block(j, masked):
        start = pl.multiple_of(j * tk, tk)
        if masked:
            qpos = i * tq + lax.broadcasted_iota(jnp.int32, (tq, tk), 0)
            kpos = j * tk + lax.broadcasted_iota(jnp.int32, (tq, tk), 1)
            visible = kpos <= qpos
        for h in range(heads):
            sl = slice(h * hd, (h + 1) * hd)
            s = _dot_nt(q_ref[:, sl], k_ref[pl.ds(start, tk), sl]) - ck_ref[h, pl.ds(j, 1), :]
            if masked:
                s = jnp.where(visible, s, NEG_BIG)
            m_old = m_scr[h]
            m_new = jnp.maximum(m_old, jnp.max(s, axis=-1, keepdims=True))
            alpha = jnp.exp2(m_old - m_new)
            p = jnp.exp2(s - jnp.tile(m_new, (1, rep)))
            l_scr[h] = alpha * l_scr[h] + jnp.sum(p, axis=-1, keepdims=True)
            acc_scr[h] = alpha * acc_scr[h] + _dot(p.astype(BF16), v_ref[pl.ds(start, tk), sl])
            m_scr[h] = m_new

    n_full = (i * tq) // tk

    def pair_step(jj, c):
        block(2 * jj, False)
        block(2 * jj + 1, False)
        return c

    lax.fori_loop(0, n_full // 2, pair_step, 0)

    @pl.when(n_full % 2 == 1)
    def _():
        block(n_full - 1, False)

    block(n_full, True)
    for h in range(heads):
        sl = slice(h * hd, (h + 1) * hd)
        o = acc_scr[h] / l_scr[h]
        o_ref[:, sl] = (o * gate_ref[:, sl].astype(F32)).astype(o_ref.dtype)


def _fox_attention(qk, v, gate, ck_rows, *, tq=512, tk=FOX_KEY_BLOCK, heads=2):
    B, S, D = v.shape
    H = D // FOX_HEAD_DIM
    HG = H // heads
    W = heads * FOX_HEAD_DIM
    tk = min(tk, S)
    tq = min(tq, tk)
    assert tk % tq == 0 and S % tk == 0 and H % heads == 0
    body = functools.partial(_fox_attn_body, tq=tq, tk=tk, heads=heads)
    return pl.pallas_call(
        body,
        grid=(B, HG, S // tq),
        in_specs=[pl.BlockSpec((None, tq, W), lambda b, h, i: (b, i, h)),
                  pl.BlockSpec((None, S, W), lambda b, h, i: (b, 0, HG + h)),
                  pl.BlockSpec((None, S, W), lambda b, h, i: (b, 0, h)),
                  pl.BlockSpec((None, tq, W), lambda b, h, i: (b, i, h)),
                  pl.BlockSpec((None, heads, S // tk, tk), lambda b, h, i: (b, h, 0, 0))],
        out_specs=pl.BlockSpec((None, tq, W), lambda b, h, i: (b, i, h)),
        out_shape=jax.ShapeDtypeStruct((B, S, D), BF16),
        scratch_shapes=[pltpu.VMEM((heads, tq, LANES), F32), pltpu.VMEM((heads, tq, LANES), F32),
                        pltpu.VMEM((heads, tq, FOX_HEAD_DIM), F32)],
        compiler_params=_cparams("parallel", "parallel", "arbitrary"),
        name="fox_attention",
    )(qk, qk, v, gate, ck_rows)


def _cross_attn_body(q_ref, k_ref, v_ref, o_ref, *, heads):
    dh = q_ref.shape[-1] // heads
    for h in range(heads):
        sl = slice(h * dh, (h + 1) * dh)
        s = _dot_nt(q_ref[:, sl], k_ref[:, sl])
        s = s - jnp.max(s, axis=-1, keepdims=True)
        p = jnp.exp(s)
        l = jnp.sum(p, axis=-1, keepdims=True)
        o = _dot(p.astype(BF16), v_ref[:, sl])
        o_ref[:, sl] = (o / l).astype(o_ref.dtype)


def _cross_attention(q, kv_all, layer, *, tq=512):
    B, S, D = q.shape
    Mem = kv_all.shape[1]
    tq = min(tq, S)
    body = functools.partial(_cross_attn_body, heads=MEM_HEADS)
    return pl.pallas_call(
        body,
        grid=(B, S // tq),
        in_specs=[pl.BlockSpec((None, tq, D), lambda b, i: (b, i, 0)),
                  pl.BlockSpec((None, Mem, D), lambda b, i: (b, 0, 2 * layer)),
                  pl.BlockSpec((None, Mem, D), lambda b, i: (b, 0, 2 * layer + 1))],
        out_specs=pl.BlockSpec((None, tq, D), lambda b, i: (b, i, 0)),
        out_shape=jax.ShapeDtypeStruct((B, S, D), BF16),
        compiler_params=_cparams("parallel", "parallel"),
        name="cross_attention",
    )(q, kv_all, kv_all)


def _conformer_body(z_ref, zp_ref, w_ref, b_ref, g_ref, beta_ref, o_ref, zs, ys, *, tm, width):
    i = pl.program_id(1)
    zs[pl.ds(CONV_HALO, tm), :] = z_ref[...]
    zs[pl.ds(0, CONV_HALO), :] = jnp.where(i == 0, 0.0, zp_ref[...])
    D = z_ref.shape[-1]
    base = CONV_HALO - (width - 1)
    for c in range(D // LANES):
        sl = slice(c * LANES, (c + 1) * LANES)
        acc = jnp.broadcast_to(b_ref[:, sl], (tm, LANES))
        for k in range(width):
            acc = acc + w_ref[k:k + 1, sl] * zs[pl.ds(base + k, tm), sl]
        ys[:, sl] = acc
    y = ys[...]
    mu = jnp.mean(y, axis=-1, keepdims=True)
    d = y - mu
    var = jnp.mean(d * d, axis=-1, keepdims=True)
    t = d * lax.rsqrt(var + LN_EPS) * g_ref[...] + beta_ref[...]
    o_ref[...] = (t * _sigmoid(t)).astype(o_ref.dtype)


def _conformer_conv(z, dw_w, dw_b, ln_g, ln_b, *, tm=256):
    B, S, D = z.shape
    width = dw_w.shape[0]
    tm = min(tm, S)
    assert width - 1 <= CONV_HALO and tm % CONV_HALO == 0 and S % tm == 0
    r = tm // CONV_HALO
    body = functools.partial(_conformer_body, tm=tm, width=width)
    row = lambda a: a.reshape(1, D)
    return pl.pallas_call(
        body,
        grid=(B, S // tm),
        in_specs=[pl.BlockSpec((None, tm, D), lambda b, i: (b, i, 0)),
                  pl.BlockSpec((None, CONV_HALO, D), lambda b, i: (b, jnp.maximum(i * r - 1, 0), 0)),
                  pl.BlockSpec((width, D), lambda b, i: (0, 0)),
                  pl.BlockSpec((1, D), lambda b, i: (0, 0)),
                  pl.BlockSpec((1, D), lambda b, i: (0, 0)),
                  pl.BlockSpec((1, D), lambda b, i: (0, 0))],
        out_specs=pl.BlockSpec((None, tm, D), lambda b, i: (b, i, 0)),
        out_shape=jax.ShapeDtypeStruct((B, S, D), BF16),
        scratch_shapes=[pltpu.VMEM((tm + CONV_HALO, D), F32), pltpu.VMEM((tm, D), F32)],
        compiler_params=_cparams("parallel", "parallel"),
        name="conformer_conv_ln",
    )(z, z, dw_w, row(dw_b), row(ln_g), row(ln_b))


def _head_sum(x, head_dim):
    n = x.shape[-1]
    r = lax.broadcasted_iota(jnp.int32, (LANES, LANES), 0) // head_dim
    c = lax.broadcasted_iota(jnp.int32, (LANES, LANES), 1) // head_dim
    ones = (r == c).astype(BF16)
    parts = []
    for j in range(n // LANES):
        parts.append(_dot_exact_rhs(x[:, j * LANES:(j + 1) * LANES], ones))
    return parts[0] if len(parts) == 1 else jnp.concatenate(parts, axis=-1)


def _rwkv_prep_body(p_ref, w0_ref, db_ref, a0_ref, ib_ref, gb_ref, kk_ref, ka_ref, rk_ref,
                    r_o, k_o, v_o, kk_o, b_o, lw_o, bonus_o, g_o, *, D):
    lw_pad = LANES
    r = p_ref[:, 0:D]
    k = p_ref[:, D:2 * D]
    v = p_ref[:, 2 * D:3 * D]
    o = 3 * D
    w_lo = p_ref[:, o:o + lw_pad]
    a_lo = p_ref[:, o + lw_pad:o + 2 * lw_pad]
    g_lo = p_ref[:, o + 2 * lw_pad:o + 2 * lw_pad + GATE_LORA]

    w_arg = w0_ref[...] + _dot(jnp.tanh(w_lo).astype(BF16), db_ref[...])
    softplus = jnp.maximum(-w_arg, 0.0) + jnp.log(1.0 + jnp.exp(-jnp.abs(w_arg)))
    lw_o[...] = -jnp.exp(-softplus - 0.5)
    a = _sigmoid(a0_ref[...] + _dot(a_lo.astype(BF16), ib_ref[...]))
    g_o[...] = _dot(_sigmoid(g_lo).astype(BF16), gb_ref[...])
    kk = k * kk_ref[...]
    nrm = jnp.sqrt(_head_sum(kk * kk, RWKV_HEAD_DIM))
    kk = kk / jnp.maximum(nrm, 1e-12)
    k2 = k * (1.0 + (a - 1.0) * ka_ref[...])
    r_o[...] = r
    k_o[...] = k2
    v_o[...] = v
    kk_o[...] = kk
    b_o[...] = kk * a
    bonus_o[...] = _head_sum(r * k2 * rk_ref[...], RWKV_HEAD_DIM) * v


def _rwkv_prep(proj, w0, decay_b, a0, iclr_b, gate_b, k_k, k_a, r_k, *, D, tm=256):
    M, P = proj.shape
    tm = min(tm, M)
    body = functools.partial(_rwkv_prep_body, D=D)
    row = lambda a: a.reshape(1, D)
    full = lambda a: pl.BlockSpec(a.shape, lambda m: (0, 0))
    ins = [proj, row(w0), decay_b, row(a0), iclr_b, gate_b, row(k_k), row(k_a), row(r_k)]
    out_spec = pl.BlockSpec((tm, D), lambda m: (m, 0))
    return pl.pallas_call(
        body,
        grid=(M // tm,),
        in_specs=[pl.BlockSpec((tm, P), lambda m: (m, 0))] + [full(a) for a in ins[1:]],
        out_specs=[out_spec] * 8,
        out_shape=[jax.ShapeDtypeStruct((M, D), F32)] * 8,
        compiler_params=_cparams("parallel"),
        name="rwkv_prep",
    )(*ins)


def _rwkv_scan_body(r_ref, k_ref, v_ref, kk_ref, b_ref, lw_ref, bonus_ref, g_ref, lnw_ref, lnb_ref,
                    o_ref, state, *, C, D):
    c_idx = pl.program_id(0)

    @pl.when(c_idx == 0)
    def _():
        state[...] = jnp.zeros_like(state)

    N = RWKV_HEAD_DIM
    GH = RWKV_GROUP_HEADS
    GW = GH * N
    n_groups = D // GW
    n_batch = r_ref.shape[0]
    bf = lambda a: a.astype(BF16)

    row = lax.broadcasted_iota(jnp.int32, (C, GW), 0)
    lane = lax.broadcasted_iota(jnp.int32, (C, GW), 1)
    strict = (lane % N) < row
    incl = (lane % N) <= row
    eye = ((lane % N) == row).astype(F32)
    head_of_lane = lax.broadcasted_iota(jnp.int32, (1, GW), 1) // N
    bd_mask = (lax.broadcasted_iota(jnp.int32, (GW, GW), 0) // N) == (lax.broadcasted_iota(jnp.int32, (GW, GW), 1) // N)
    tri = (lax.broadcasted_iota(jnp.int32, (C, C), 1) <= lax.broadcasted_iota(jnp.int32, (C, C), 0)).astype(BF16)

    def block_diag(x):
        return jnp.concatenate([jnp.where(head_of_lane == h, x, jnp.zeros_like(x)) for h in range(GH)], axis=0)

    units = [(bi, g) for bi in range(n_batch) for g in range(n_groups)]
    gsl = lambda g: slice(g * GW, (g + 1) * GW)

    pre = []
    for bi in range(n_batch):
        lw = lw_ref[bi]
        cum = _dot_exact_lhs(tri, lw)
        cum_last = cum[C - 1:C, :]
        g_inv = jnp.exp(-cum)
        g_tail = jnp.exp(cum_last - cum)
        kf = k_ref[bi]
        bfl = b_ref[bi]
        pre.append(dict(
            r_t=bf(r_ref[bi] * jnp.exp(cum)), kk_t=bf(kk_ref[bi] * jnp.exp(cum - lw)),
            k_t=bf(kf * g_inv), b_t=bf(bfl * g_inv), k_h=bf(kf * g_tail), b_h=bf(bfl * g_tail),
            v=bf(v_ref[bi]), g_last=jnp.exp(cum_last)))

    lhs_kr, Ls, Akks, RKs, Vbd = [], [], [], [], []
    for bi, g in units:
        p = pre[bi]
        lhs = jnp.concatenate([p["kk_t"][:, gsl(g)], p["r_t"][:, gsl(g)]], axis=0)
        ab = _dot_nt(lhs, block_diag(p["b_t"][:, gsl(g)]))
        ak = _dot_nt(lhs, block_diag(p["k_t"][:, gsl(g)]))
        lhs_kr.append(lhs)
        Ls.append(jnp.where(strict, ab[0:C], 0.0))
        Akks.append(bf(jnp.where(strict, ak[0:C], 0.0)))
        RKs.append(jnp.concatenate([bf(jnp.where(incl, ab[C:2 * C], 0.0)),
                                    bf(jnp.where(incl, ak[C:2 * C], 0.0))], axis=1))
        Vbd.append(block_diag(p["v"][:, gsl(g)]))
    AkkVs = [_dot(a, vb) for a, vb in zip(Akks, Vbd)]

    Xs = [eye - L for L in Ls]
    Qs = [_dot(bf(L), block_diag(bf(L))) for L in Ls]
    span = 4
    while span < C:
        nxt = [_dot(jnp.concatenate([bf(Q), bf(X)], axis=0), block_diag(bf(Q))) for Q, X in zip(Qs, Xs)]
        Qs = [n[0:C] for n in nxt]
        Xs = [X + n[C:2 * C] for X, n in zip(Xs, nxt)]
        span *= 2
    Xs = [X + _dot(bf(X), block_diag(bf(Q))) for Q, X in zip(Qs, Xs)]

    KRS = [_dot_nt(lhs, bf(state[bi, g])) for lhs, (bi, g) in zip(lhs_kr, units)]
    Es = [-_dot(bf(X), block_diag(bf(krs[0:C] + akkv))) for X, krs, akkv in zip(Xs, KRS, AkkVs)]
    ys = [[None] * n_groups for _ in range(n_batch)]
    for i, (bi, g) in enumerate(units):
        p = pre[bi]
        e16 = bf(Es[i])
        ys[bi][g] = KRS[i][C:2 * C] + _dot(RKs[i], jnp.concatenate([block_diag(e16), Vbd[i]], axis=0))
        upd = _dot_tn(jnp.concatenate([e16, p["v"][:, gsl(g)]], axis=0),
                      jnp.concatenate([p["b_h"][:, gsl(g)], p["k_h"][:, gsl(g)]], axis=0))
        state[bi, g] = state[bi, g] * p["g_last"][:, gsl(g)] + jnp.where(bd_mask, upd, 0.0)

    for bi in range(n_batch):
        y = jnp.concatenate(ys[bi], axis=-1)
        mean = _head_sum(y, N) * (1.0 / N)
        d = y - mean
        var = _head_sum(d * d, N) * (1.0 / N)
        yn = d * lax.rsqrt(var + RWKV_GN_EPS) * lnw_ref[...] + lnb_ref[...] + bonus_ref[bi]
        o_ref[bi] = (yn * g_ref[bi]).astype(o_ref.dtype)


def _rwkv_scan(r, k, v, kk, b, lw, bonus, g, ln_w, ln_b, *, B, S):
    M, D = r.shape
    C = RWKV_CHUNK
    assert S % C == 0 and C == RWKV_HEAD_DIM and D % (RWKV_GROUP_HEADS * RWKV_HEAD_DIM) == 0
    nc = S // C
    GW = RWKV_GROUP_HEADS * RWKV_HEAD_DIM
    body = functools.partial(_rwkv_scan_body, C=C, D=D)
    tile = pl.BlockSpec((B, C, D), lambda c: (0, c, 0))
    vec = pl.BlockSpec((1, D), lambda c: (0, 0))
    seq = lambda a: a.reshape(B, S, D)
    return pl.pallas_call(
        body,
        grid=(nc,),
        in_specs=[tile] * 8 + [vec, vec],
        out_specs=tile,
        out_shape=jax.ShapeDtypeStruct((B, S, D), BF16),
        scratch_shapes=[pltpu.VMEM((B, D // GW, GW, GW), F32)],
        compiler_params=_cparams("arbitrary"),
        name="rwkv_scan",
    )(*(seq(a) for a in (r, k, v, kk, b, lw, bonus, g)), ln_w.reshape(1, D), ln_b.reshape(1, D)).reshape(M, D)


def _rmsnorm_body(x_ref, g_ref, o_ref):
    o_ref[...] = _rms_rows(x_ref[...], g_ref[...])


def _rmsnorm(x2d, gain_row, *, tm=512):
    M, K = x2d.shape
    tm = min(tm, M)
    return pl.pallas_call(
        _rmsnorm_body,
        grid=(M // tm,),
        in_specs=[pl.BlockSpec((tm, K), lambda m: (m, 0)), pl.BlockSpec((1, K), lambda m: (0, 0))],
        out_specs=pl.BlockSpec((tm, K), lambda m: (m, 0)),
        out_shape=jax.ShapeDtypeStruct((M, K), F32),
        compiler_params=_cparams("parallel"),
        name="final_rmsnorm",
    )(x2d, gain_row)


def _pad_cols(a, n):
    return jnp.pad(a, ((0, 0), (0, n - a.shape[1])))


def _pad_rows(a, n):
    return jnp.pad(a, ((0, n - a.shape[0]), (0, 0)))


def _fox_layer(x2d, B, S, gain_row, w_in, b_f, q_gain, k_gain, w_o):
    M, D = x2d.shape
    H = FOX_HEADS
    w_f = _pad_cols(w_in[:, 4 * D:], LANES).astype(BF16)
    b_row = _pad_cols(b_f.reshape(1, H), LANES)
    scale = FOX_HEAD_DIM ** -0.5
    gains = jnp.concatenate([jnp.tile(q_gain * (scale * LOG2E), H), jnp.tile(k_gain, H)]).reshape(1, 2 * D)
    proj = functools.partial(_proj, x2d, gain_row, seq=S, out_dtype=BF16, tn=1024)
    qk = proj([w_in[:, :2 * D].astype(BF16)], epilogue="head_norm", extras=(gains,))
    v = proj([w_in[:, 2 * D:3 * D].astype(BF16)], epilogue="scale")
    gate = proj([w_in[:, 3 * D:4 * D].astype(BF16)], epilogue="sigmoid")
    c = _forget_cumsum(x2d.reshape(B, S, D), gain_row, w_f, b_row)
    tk = min(FOX_KEY_BLOCK, S)
    ck_rows = c[:, :, :H].transpose(0, 2, 1).reshape(B, H, S // tk, tk)
    o = _fox_attention(qk.reshape(B, S, 2 * D), v.reshape(B, S, D), gate.reshape(B, S, D), ck_rows, tk=tk)
    return _matmul_residual(o.reshape(M, D), w_o.astype(BF16), x2d)


def _rwkv_layer(x2d, B, S, gain_row, w_in, mu, w0, decay_b, a0, iclr_b, gate_b, k_k, k_a, r_k,
                ln_w, ln_b, w_o):
    M, D = x2d.shape
    o3 = 3 * D
    segs = [(0, o3, o3), (o3, DECAY_LORA, LANES), (o3 + DECAY_LORA, ICLR_LORA, LANES),
            (o3 + DECAY_LORA + ICLR_LORA, GATE_LORA, GATE_LORA)]
    w_pad = jnp.concatenate([_pad_cols(w_in[:, s:s + n], p) for s, n, p in segs], axis=1)
    mu_pad = jnp.concatenate([_pad_cols(mu[None, s:s + n], p) for s, n, p in segs], axis=1)
    P = w_pad.shape[1]
    tn = 512
    Pp = -(-P // tn) * tn
    w_pad = _pad_cols(w_pad, Pp).astype(BF16)
    mu_pad = _pad_cols(mu_pad, Pp)
    proj = _proj(x2d, gain_row, [w_pad], seq=S, epilogue="lerp", out_dtype=F32, extras=(mu_pad,), tn=tn)
    r, k, v, kk, b, lw, bonus, g = _rwkv_prep(
        proj, w0, _pad_rows(decay_b, LANES).astype(BF16), a0, _pad_rows(iclr_b, LANES).astype(BF16),
        gate_b.astype(BF16), k_k, k_a, r_k.reshape(-1), D=D)
    y = _rwkv_scan(r, k, v, kk, b, lw, bonus, g, ln_w, ln_b, B=B, S=S)
    return _matmul_residual(y, w_o.astype(BF16), x2d)


def _conformer_layer(x2d, B, S, gain_row, w_in, dw_w, dw_b, ln_g, ln_b, w_o):
    M, D = x2d.shape
    tn = 512
    z = _proj(x2d, gain_row, [w_in.astype(BF16)], seq=S, epilogue="glu", out_dtype=F32,
              w_col_offsets=[0, D // tn], tn=tn)
    t = _conformer_conv(z.reshape(B, S, D), dw_w, dw_b, ln_g, ln_b)
    return _matmul_residual(t.reshape(M, D), w_o.astype(BF16), x2d)


def _cross_layer(x2d, B, S, gain_row, kv_all, layer, w_q, w_o):
    M, D = x2d.shape
    scale = (D // MEM_HEADS) ** -0.5
    q = _proj(x2d, gain_row, [w_q.astype(BF16)], seq=S, epilogue="scale", out_dtype=BF16, scale=scale, tn=1024)
    o = _cross_attention(q.reshape(B, S, D), kv_all, layer)
    return _matmul_residual(o.reshape(M, D), w_o.astype(BF16), x2d)


def _ffn_layer(x2d, S, gain_row, w_up, dw_w, dw_b, w_down):
    d_ff = w_down.shape[0]
    tn = 512
    act = _proj(x2d, gain_row, [w_up.astype(BF16)], seq=S, epilogue="ffn", out_dtype=BF16,
                extras=(dw_w, dw_b.reshape(1, -1)), w_col_offsets=[0, d_ff // tn], tn=tn)
    return _matmul_residual(act, w_down.astype(BF16), x2d)


def kernel(x, mem, mem_norm, norm_mix, norm_cross, norm_ffn, final_norm, fox_w_in, fox_b_f, fox_q_gain, fox_k_gain, fox_w_o, rwkv_w_in, rwkv_mu, rwkv_w0, rwkv_decay_b, rwkv_a0, rwkv_iclr_b, rwkv_gate_b, rwkv_k_k, rwkv_k_a, rwkv_r_k, rwkv_ln_w, rwkv_ln_b, rwkv_w_o, conv_w_in, conv_dw_w, conv_dw_b, conv_ln_g, conv_ln_b, conv_w_o, cross_w_q, cross_w_kv, cross_w_o, ffn_w_up, ffn_dw_w, ffn_dw_b, ffn_w_down):
    B, S, D = x.shape
    depth = norm_mix.shape[0]
    Mem = mem.shape[1]
    M = B * S
    x2d = x.reshape(M, D)

    w_kv_all = jnp.concatenate([cross_w_kv[i] for i in range(depth)], axis=1).astype(BF16)
    kv_all = _proj(mem.reshape(B * Mem, D), mem_norm.reshape(1, D), [w_kv_all], seq=Mem,
                   epilogue="scale", out_dtype=BF16).reshape(B, Mem, depth * 2 * D)

    for i in range(depth):
        kind, j = i % 3, i // 3
        g_mix = norm_mix[i].reshape(1, D)
        if kind == 0:
            x2d = _fox_layer(x2d, B, S, g_mix, fox_w_in[j], fox_b_f[j], fox_q_gain[j], fox_k_gain[j],
                             fox_w_o[j])
        elif kind == 1:
            x2d = _rwkv_layer(x2d, B, S, g_mix, rwkv_w_in[j], rwkv_mu[j], rwkv_w0[j], rwkv_decay_b[j],
                              rwkv_a0[j], rwkv_iclr_b[j], rwkv_gate_b[j], rwkv_k_k[j], rwkv_k_a[j],
                              rwkv_r_k[j], rwkv_ln_w[j], rwkv_ln_b[j], rwkv_w_o[j])
        else:
            x2d = _conformer_layer(x2d, B, S, g_mix, conv_w_in[j], conv_dw_w[j], conv_dw_b[j],
                                   conv_ln_g[j], conv_ln_b[j], conv_w_o[j])
        x2d = _cross_layer(x2d, B, S, norm_cross[i].reshape(1, D), kv_all, i, cross_w_q[i], cross_w_o[i])
        x2d = _ffn_layer(x2d, S, norm_ffn[i].reshape(1, D), ffn_w_up[i], ffn_dw_w[i], ffn_dw_b[i],
                         ffn_w_down[i])
    return _rmsnorm(x2d, final_norm.reshape(1, D)).reshape(B, S, D)
```

```python
import functools

import jax
import jax.numpy as jnp
from jax import lax
from jax.experimental import pallas as pl
from jax.experimental.pallas import tpu as pltpu

F32 = jnp.float32
BF16 = jnp.bfloat16

RMS_EPS = 1e-6
LN_EPS = 1e-5
RWKV_GN_EPS = 64e-5

LANES = 128
HALO = 16
SUB_COLS = 256
VMEM_LIMIT = 56 * 1024 * 1024
CAST_BLOCK_BYTES = 4 * 1024 * 1024

FOX_HEADS = 16
FOX_HEAD_DIM = 128
FOX_KEY_BLOCK = 1024
MEM_HEADS = 4
RWKV_HEAD_DIM = 64
RWKV_CHUNK = 64
RWKV_GROUP_HEADS = 4
DECAY_LORA = 96
ICLR_LORA = 96
GATE_LORA = 256
CONV_HALO = 32
NEG_BIG = -1e30
LOG2E = 1.4426950408889634


def _cparams(*sem):
    return pltpu.CompilerParams(dimension_semantics=sem, vmem_limit_bytes=VMEM_LIMIT)


def _rms_rows(x, g):
    ms = jnp.mean(x * x, axis=-1, keepdims=True)
    return x * lax.rsqrt(ms + RMS_EPS) * g


def _sigmoid(x):
    return 0.5 * jnp.tanh(0.5 * x) + 0.5


def _dot(a, b):
    return jnp.dot(a, b, preferred_element_type=F32)


def _dot_nt(a, b):
    return lax.dot_general(a, b, (((1,), (1,)), ((), ())), preferred_element_type=F32)


def _dot_tn(a, b):
    return lax.dot_general(a, b, (((0,), (0,)), ((), ())), preferred_element_type=F32)


def _split3(x):
    hi = x.astype(BF16)
    r1 = x - hi.astype(F32)
    mid = r1.astype(BF16)
    lo = (r1 - mid.astype(F32)).astype(BF16)
    return hi, mid, lo


def _dot_exact_lhs(sel, x):
    hi, mid, lo = _split3(x)
    return _dot(sel, hi) + _dot(sel, mid) + _dot(sel, lo)


def _dot_exact_rhs(x, sel):
    hi, mid, lo = _split3(x)
    return _dot(hi, sel) + _dot(mid, sel) + _dot(lo, sel)


def _proj_body(*refs, n_w, halo, epilogue, tm, tn, tiles_per_seq, scale):
    it = iter(refs)
    x_ref = next(it)
    xp_ref = next(it) if halo else None
    g_ref = next(it)
    w_refs = [next(it) for _ in range(n_w)]
    if epilogue == "ffn":
        cw_refs = [next(it) for _ in range(n_w)]
        cb_refs = [next(it) for _ in range(n_w)]
    elif epilogue in ("lerp", "head_norm"):
        vec_ref = next(it)
    out_ref = next(it)
    h_scr = next(it)
    acc_scr = next(it)

    m = pl.program_id(0)
    n = pl.program_id(1)
    off = HALO if halo else 0

    @pl.when(n == 0)
    def _():
        g = g_ref[...]
        h_scr[pl.ds(off, tm), :] = _rms_rows(x_ref[...], g).astype(BF16)
        if halo:
            hp = _rms_rows(xp_ref[...], g)
            first = (m % tiles_per_seq) == 0
            h_scr[pl.ds(0, HALO), :] = jnp.where(first, 0.0, hp).astype(BF16)

    sub = min(SUB_COLS, tn)
    col_slices = [slice(c * sub, (c + 1) * sub) for c in range(tn // sub)]

    def store(cs, val):
        out_ref[:, cs] = val.astype(out_ref.dtype)

    def acc_rows(wi, cs, shift=0):
        return acc_scr[wi, pl.ds(off - shift, tm), cs]

    def epilogue_tile(cs):
        if epilogue == "scale":
            store(cs, acc_rows(0, cs) * scale)
        elif epilogue == "sigmoid":
            store(cs, _sigmoid(acc_rows(0, cs)))
        elif epilogue == "glu":
            store(cs, acc_rows(0, cs) * _sigmoid(acc_rows(1, cs)))
        elif epilogue == "head_norm":
            for j in range(sub // FOX_HEAD_DIM):
                lo = cs.start + j * FOX_HEAD_DIM
                sl = slice(lo, lo + FOX_HEAD_DIM)
                store(sl, _rms_rows(acc_rows(0, sl), vec_ref[:, sl]))
        elif epilogue == "ffn":
            ys = []
            for wi, (cw_ref, cb_ref) in enumerate(zip(cw_refs, cb_refs)):
                cw = cw_ref[:, cs]
                taps = cw.shape[0]
                y = cb_ref[:, cs] + cw[taps - 1:taps, :] * acc_rows(wi, cs)
                for j in range(taps - 1):
                    y = y + cw[j:j + 1, :] * acc_rows(wi, cs, shift=taps - 1 - j)
                ys.append(y)
            u, gte = ys
            store(cs, gte * _sigmoid(gte) * u)
        elif epilogue == "lerp":
            cur = acc_rows(0, cs)
            store(cs, cur + (acc_rows(0, cs, shift=1) - cur) * vec_ref[:, cs])
        else:
            raise ValueError(epilogue)

    h = h_scr[...]
    for cs in col_slices:
        for wi, w_ref in enumerate(w_refs):
            acc_scr[wi, :, cs] = _dot(h, w_ref[:, cs])
        epilogue_tile(cs)


def _proj(x2d, gain_row, ws, *, seq, epilogue, out_dtype, extras=(), scale=1.0, tm=1024, tn=512,
          w_col_offsets=None, n_out=None):
    M, K = x2d.shape
    n_w = len(ws) if w_col_offsets is None else len(w_col_offsets)
    halo = epilogue in ("ffn", "lerp")
    tm = min(tm, M, seq)
    assert M % tm == 0 and seq % tm == 0 and tm % HALO == 0
    if w_col_offsets is None:
        w_col_offsets = [0] * n_w
        n_out = ws[0].shape[1]
        w_list = list(ws)
    else:
        w_list = [ws[0]] * n_w
        n_out = n_out or ws[0].shape[1] // n_w
    tn = min(tn, n_out)
    assert n_out % tn == 0

    def cols(o):
        return lambda m, n: (0, n + o)

    in_specs = [pl.BlockSpec((tm, K), lambda m, n: (m, 0))]
    args = [x2d]
    if halo:
        r = tm // HALO
        in_specs.append(pl.BlockSpec((HALO, K), lambda m, n: (jnp.maximum(m * r - 1, 0), 0)))
        args.append(x2d)
    in_specs.append(pl.BlockSpec((1, K), lambda m, n: (0, 0)))
    args.append(gain_row)
    for w, o in zip(w_list, w_col_offsets):
        in_specs.append(pl.BlockSpec((K, tn), cols(o)))
        args.append(w)
    if epilogue == "ffn":
        cw, cb = extras
        taps = cw.shape[0]
        assert taps - 1 <= HALO
        for o in w_col_offsets:
            in_specs.append(pl.BlockSpec((taps, tn), cols(o)))
            args.append(cw)
        for o in w_col_offsets:
            in_specs.append(pl.BlockSpec((1, tn), cols(o)))
            args.append(cb)
    elif epilogue in ("lerp", "head_norm"):
        in_specs.append(pl.BlockSpec((1, tn), cols(0)))
        args.append(extras[0])

    rows = tm + (HALO if halo else 0)
    scratch = [pltpu.VMEM((rows, K), BF16), pltpu.VMEM((n_w, rows, tn), F32)]

    body = functools.partial(_proj_body, n_w=n_w, halo=halo, epilogue=epilogue, tm=tm, tn=tn,
                             tiles_per_seq=seq // tm, scale=scale)
    return pl.pallas_call(
        body,
        grid=(M // tm, n_out // tn),
        in_specs=in_specs,
        out_specs=pl.BlockSpec((tm, tn), lambda m, n: (m, n)),
        out_shape=jax.ShapeDtypeStruct((M, n_out), out_dtype),
        scratch_shapes=scratch,
        compiler_params=_cparams("parallel", "arbitrary"),
        name="proj_" + epilogue,
    )(*args)


def _mm_res_body(a_ref, w_ref, r_ref, o_ref):
    a = a_ref[...]
    sub = min(SUB_COLS, o_ref.shape[-1])
    for c in range(o_ref.shape[-1] // sub):
        cs = slice(c * sub, (c + 1) * sub)
        o_ref[:, cs] = r_ref[:, cs] + _dot(a, w_ref[:, cs])


def _matmul_residual(a, w, res):
    M, K = a.shape
    N = w.shape[1]
    tm, tn = (512, 2048) if K <= 2048 else (1024, 512)
    tm, tn = min(tm, M), min(tn, N)
    assert M % tm == 0 and N % tn == 0
    return pl.pallas_call(
        _mm_res_body,
        grid=(M // tm, N // tn),
        in_specs=[pl.BlockSpec((tm, K), lambda m, n: (m, 0)),
                  pl.BlockSpec((K, tn), lambda m, n: (0, n)),
                  pl.BlockSpec((tm, tn), lambda m, n: (m, n))],
        out_specs=pl.BlockSpec((tm, tn), lambda m, n: (m, n)),
        out_shape=jax.ShapeDtypeStruct((M, N), F32),
        compiler_params=_cparams("parallel", "parallel"),
        name="matmul_residual",
    )(a, w, res)


def _forget_body(x_ref, g_ref, w_ref, b_ref, c_ref, carry):
    s = pl.program_id(1)

    @pl.when(s == 0)
    def _():
        carry[...] = jnp.zeros_like(carry)

    h = _rms_rows(x_ref[...], g_ref[...]).astype(BF16)
    z = _dot(h, w_ref[...]) + b_ref[...]
    logf = jnp.minimum(z, 0.0) - jnp.log(1.0 + jnp.exp(-jnp.abs(z)))
    tm = logf.shape[0]
    row = lax.broadcasted_iota(jnp.int32, (tm, tm), 0)
    col = lax.broadcasted_iota(jnp.int32, (tm, tm), 1)
    tri = (col <= row).astype(BF16)
    c = _dot_exact_lhs(tri, logf) + carry[...]
    c_ref[...] = c * LOG2E
    carry[...] = c[tm - 1:tm, :]


def _forget_cumsum(x, gain_row, w_f, b_f, *, tm=256):
    B, S, K = x.shape
    tm = min(tm, S)
    return pl.pallas_call(
        _forget_body,
        grid=(B, S // tm),
        in_specs=[pl.BlockSpec((None, tm, K), lambda b, s: (b, s, 0)),
                  pl.BlockSpec((1, K), lambda b, s: (0, 0)),
                  pl.BlockSpec((K, LANES), lambda b, s: (0, 0)),
                  pl.BlockSpec((1, LANES), lambda b, s: (0, 0))],
        out_specs=pl.BlockSpec((None, tm, LANES), lambda b, s: (b, s, 0)),
        out_shape=jax.ShapeDtypeStruct((B, S, LANES), F32),
        scratch_shapes=[pltpu.VMEM((1, LANES), F32)],
        compiler_params=_cparams("parallel", "arbitrary"),
        name="fox_forget_cumsum",
    )(x, gain_row, w_f, b_f)


def _fox_attn_body(q_ref, k_ref, v_ref, gate_ref, ck_ref, o_ref, m_scr, l_scr, acc_scr, *, tq, tk, heads):
    i = pl.program_id(2)
    hd = FOX_HEAD_DIM
    m_scr[...] = jnp.full_like(m_scr, NEG_BIG)
    l_scr[...] = jnp.zeros_like(l_scr)
    acc_scr[...] = jnp.zeros_like(acc_scr)

    def block(j, masked, lo=0, width=tk):
        start = pl.multiple_of(j * tk, tk) + lo
        if masked:
            qpos = i * tq + lax.broadcasted_iota(jnp.int32, (tq, width), 0)
            kpos = j * tk + lo + lax.broadcasted_iota(jnp.int32, (tq, width), 1)
            visible = kpos <= qpos
        for h in range(heads):
            sl = slice(h * hd, (h + 1) * hd)
            s = _dot_nt(q_ref[:, sl], k_ref[pl.ds(start, width), sl]) - ck_ref[h, pl.ds(j, 1), lo:lo + width]
            if masked:
                s = jnp.where(visible, s, NEG_BIG)
            m_old = m_scr[h]
            m_new = jnp.maximum(m_old, jnp.max(s, axis=-1, keepdims=True))
            alpha = jnp.exp2(m_old - m_new)
            p = jnp.exp2(s - jnp.tile(m_new, (1, width // LANES)))
            l_scr[h] = alpha * l_scr[h] + jnp.sum(p, axis=-1, keepdims=True)
            acc_scr[h] = alpha * acc_scr[h] + _dot(p.astype(BF16), v_ref[pl.ds(start, width), sl])
            m_scr[h] = m_new

    n_full = (i * tq) // tk

    def pair_step(jj, c):
        block(2 * jj, False)
        block(2 * jj + 1, False)
        return c

    lax.fori_loop(0, n_full // 2, pair_step, 0)

    @pl.when(n_full % 2 == 1)
    def _():
        block(n_full - 1, False)

    if tk == 2 * tq:
        pl.when(i % 2 == 0)(lambda: block(n_full, True, 0, tq))

        @pl.when(i % 2 == 1)
        def _():
            block(n_full, False, 0, tq)
            block(n_full, True, tq, tq)
    else:
        block(n_full, True)
    for h in range(heads):
        sl = slice(h * hd, (h + 1) * hd)
        o = acc_scr[h] / l_scr[h]
        o_ref[:, sl] = (o * gate_ref[:, sl].astype(F32)).astype(o_ref.dtype)


def _fox_attention(qk, v, gate, ck_rows, *, tq=512, tk=FOX_KEY_BLOCK, heads=2):
    B, S, D = v.shape
    H = D // FOX_HEAD_DIM
    HG = H // heads
    W = heads * FOX_HEAD_DIM
    tk = min(tk, S)
    tq = min(tq, tk)
    assert tk in (tq, 2 * tq) and S % tk == 0 and H % heads == 0
    body = functools.partial(_fox_attn_body, tq=tq, tk=tk, heads=heads)
    return pl.pallas_call(
        body,
        grid=(B, HG, S // tq),
        in_specs=[pl.BlockSpec((None, tq, W), lambda b, h, i: (b, i, h)),
                  pl.BlockSpec((None, S, W), lambda b, h, i: (b, 0, HG + h)),
                  pl.BlockSpec((None, S, W), lambda b, h, i: (b, 0, h)),
                  pl.BlockSpec((None, tq, W), lambda b, h, i: (b, i, h)),
                  pl.BlockSpec((None, heads, S // tk, tk), lambda b, h, i: (b, h, 0, 0))],
        out_specs=pl.BlockSpec((None, tq, W), lambda b, h, i: (b, i, h)),
        out_shape=jax.ShapeDtypeStruct((B, S, D), BF16),
        scratch_shapes=[pltpu.VMEM((heads, tq, LANES), F32), pltpu.VMEM((heads, tq, LANES), F32),
                        pltpu.VMEM((heads, tq, FOX_HEAD_DIM), F32)],
        compiler_params=_cparams("parallel", "parallel", "arbitrary"),
        name="fox_attention",
    )(qk, qk, v, gate, ck_rows)


def _cross_attn_body(q_ref, k_ref, v_ref, o_ref, *, heads):
    dh = q_ref.shape[-1] // heads
    for h in range(heads):
        sl = slice(h * dh, (h + 1) * dh)
        s = _dot_nt(q_ref[:, sl], k_ref[:, sl])
        s = s - jnp.max(s, axis=-1, keepdims=True)
        p = jnp.exp(s)
        l = jnp.sum(p, axis=-1, keepdims=True)
        o = _dot(p.astype(BF16), v_ref[:, sl])
        o_ref[:, sl] = (o / l).astype(o_ref.dtype)


def _cross_attention(q, kv, *, tq=512):
    B, S, D = q.shape
    Mem = kv.shape[1]
    tq = min(tq, S)
    body = functools.partial(_cross_attn_body, heads=MEM_HEADS)
    return pl.pallas_call(
        body,
        grid=(B, S // tq),
        in_specs=[pl.BlockSpec((None, tq, D), lambda b, i: (b, i, 0)),
                  pl.BlockSpec((None, Mem, D), lambda b, i: (b, 0, 0)),
                  pl.BlockSpec((None, Mem, D), lambda b, i: (b, 0, 1))],
        out_specs=pl.BlockSpec((None, tq, D), lambda b, i: (b, i, 0)),
        out_shape=jax.ShapeDtypeStruct((B, S, D), BF16),
        compiler_params=_cparams("parallel", "parallel"),
        name="cross_attention",
    )(q, kv, kv)


def _conformer_body(z_ref, zp_ref, w_ref, b_ref, g_ref, beta_ref, o_ref, zs, ys, *, tm, width):
    i = pl.program_id(1)
    zs[pl.ds(CONV_HALO, tm), :] = z_ref[...]
    zs[pl.ds(0, CONV_HALO), :] = jnp.where(i == 0, 0.0, zp_ref[...])
    D = z_ref.shape[-1]
    base = CONV_HALO - (width - 1)
    for c in range(D // LANES):
        sl = slice(c * LANES, (c + 1) * LANES)
        acc = jnp.broadcast_to(b_ref[:, sl], (tm, LANES))
        for k in range(width):
            acc = acc + w_ref[k:k + 1, sl] * zs[pl.ds(base + k, tm), sl]
        ys[:, sl] = acc
    y = ys[...]
    mu = jnp.mean(y, axis=-1, keepdims=True)
    d = y - mu
    var = jnp.mean(d * d, axis=-1, keepdims=True)
    t = d * lax.rsqrt(var + LN_EPS) * g_ref[...] + beta_ref[...]
    o_ref[...] = (t * _sigmoid(t)).astype(o_ref.dtype)


def _conformer_conv(z, dw_w, dw_b, ln_g, ln_b, *, tm=256):
    B, S, D = z.shape
    width = dw_w.shape[0]
    tm = min(tm, S)
    assert width - 1 <= CONV_HALO and tm % CONV_HALO == 0 and S % tm == 0
    r = tm // CONV_HALO
    body = functools.partial(_conformer_body, tm=tm, width=width)
    row = lambda a: a.reshape(1, D)
    return pl.pallas_call(
        body,
        grid=(B, S // tm),
        in_specs=[pl.BlockSpec((None, tm, D), lambda b, i: (b, i, 0)),
                  pl.BlockSpec((None, CONV_HALO, D), lambda b, i: (b, jnp.maximum(i * r - 1, 0), 0)),
                  pl.BlockSpec((width, D), lambda b, i: (0, 0)),
                  pl.BlockSpec((1, D), lambda b, i: (0, 0)),
                  pl.BlockSpec((1, D), lambda b, i: (0, 0)),
                  pl.BlockSpec((1, D), lambda b, i: (0, 0))],
        out_specs=pl.BlockSpec((None, tm, D), lambda b, i: (b, i, 0)),
        out_shape=jax.ShapeDtypeStruct((B, S, D), BF16),
        scratch_shapes=[pltpu.VMEM((tm + CONV_HALO, D), F32), pltpu.VMEM((tm, D), F32)],
        compiler_params=_cparams("parallel", "parallel"),
        name="conformer_conv_ln",
    )(z, z, dw_w, row(dw_b), row(ln_g), row(ln_b))


def _head_sum(x, head_dim):
    n = x.shape[-1]
    r = lax.broadcasted_iota(jnp.int32, (LANES, LANES), 0) // head_dim
    c = lax.broadcasted_iota(jnp.int32, (LANES, LANES), 1) // head_dim
    ones = (r == c).astype(BF16)
    parts = []
    for j in range(n // LANES):
        parts.append(_dot_exact_rhs(x[:, j * LANES:(j + 1) * LANES], ones))
    return parts[0] if len(parts) == 1 else jnp.concatenate(parts, axis=-1)


def _rwkv_prep_body(p_ref, w0_ref, db_ref, a0_ref, ib_ref, gb_ref, kk_ref, ka_ref, rk_ref,
                    r_o, k_o, v_o, kk_o, b_o, lw_o, bonus_o, g_o, *, D):
    lw_pad = LANES
    r = p_ref[:, 0:D]
    k = p_ref[:, D:2 * D]
    v = p_ref[:, 2 * D:3 * D]
    o = 3 * D
    w_lo = p_ref[:, o:o + lw_pad]
    a_lo = p_ref[:, o + lw_pad:o + 2 * lw_pad]
    g_lo = p_ref[:, o + 2 * lw_pad:o + 2 * lw_pad + GATE_LORA]

    w_arg = w0_ref[...] + _dot(jnp.tanh(w_lo).astype(BF16), db_ref[...])
    softplus = jnp.maximum(-w_arg, 0.0) + jnp.log(1.0 + jnp.exp(-jnp.abs(w_arg)))
    lw_o[...] = -jnp.exp(-softplus - 0.5)
    a = _sigmoid(a0_ref[...] + _dot(a_lo.astype(BF16), ib_ref[...]))
    g_o[...] = _dot(_sigmoid(g_lo).astype(BF16), gb_ref[...])
    kk = k * kk_ref[...]
    nrm = jnp.sqrt(_head_sum(kk * kk, RWKV_HEAD_DIM))
    kk = kk / jnp.maximum(nrm, 1e-12)
    k2 = k * (1.0 + (a - 1.0) * ka_ref[...])
    r_o[...] = r
    k_o[...] = k2
    v_o[...] = v
    kk_o[...] = kk
    b_o[...] = kk * a
    bonus_o[...] = _head_sum(r * k2 * rk_ref[...], RWKV_HEAD_DIM) * v


def _rwkv_prep(proj, w0, decay_b, a0, iclr_b, gate_b, k_k, k_a, r_k, *, D, tm=256):
    M, P = proj.shape
    tm = min(tm, M)
    body = functools.partial(_rwkv_prep_body, D=D)
    row = lambda a: a.reshape(1, D)
    full = lambda a: pl.BlockSpec(a.shape, lambda m: (0, 0))
    ins = [proj, row(w0), decay_b, row(a0), iclr_b, gate_b, row(k_k), row(k_a), row(r_k)]
    out_spec = pl.BlockSpec((tm, D), lambda m: (m, 0))
    return pl.pallas_call(
        body,
        grid=(M // tm,),
        in_specs=[pl.BlockSpec((tm, P), lambda m: (m, 0))] + [full(a) for a in ins[1:]],
        out_specs=[out_spec] * 8,
        out_shape=[jax.ShapeDtypeStruct((M, D), F32)] * 8,
        compiler_params=_cparams("parallel"),
        name="rwkv_prep",
    )(*ins)


def _rwkv_scan_body(r_ref, k_ref, v_ref, kk_ref, b_ref, lw_ref, bonus_ref, g_ref, lnw_ref, lnb_ref,
                    o_ref, state, *, C, D):
    c_idx = pl.program_id(0)

    @pl.when(c_idx == 0)
    def _():
        state[...] = jnp.zeros_like(state)

    N = RWKV_HEAD_DIM
    GH = RWKV_GROUP_HEADS
    GW = GH * N
    n_groups = D // GW
    n_batch = r_ref.shape[0]
    bf = lambda a: a.astype(BF16)

    row = lax.broadcasted_iota(jnp.int32, (C, GW), 0)
    lane = lax.broadcasted_iota(jnp.int32, (C, GW), 1)
    strict = (lane % N) < row
    incl = (lane % N) <= row
    eye = ((lane % N) == row).astype(F32)
    head_of_lane = lax.broadcasted_iota(jnp.int32, (1, GW), 1) // N
    bd_mask = (lax.broadcasted_iota(jnp.int32, (GW, GW), 0) // N) == (lax.broadcasted_iota(jnp.int32, (GW, GW), 1) // N)
    tri = (lax.broadcasted_iota(jnp.int32, (C, C), 1) <= lax.broadcasted_iota(jnp.int32, (C, C), 0)).astype(BF16)

    def block_diag(x):
        return jnp.concatenate([jnp.where(head_of_lane == h, x, jnp.zeros_like(x)) for h in range(GH)], axis=0)

    units = [(bi, g) for bi in range(n_batch) for g in range(n_groups)]
    gsl = lambda g: slice(g * GW, (g + 1) * GW)

    pre = []
    for bi in range(n_batch):
        lw = lw_ref[bi]
        cum = _dot_exact_lhs(tri, lw)
        cum_last = cum[C - 1:C, :]
        g_inv = jnp.exp(-cum)
        g_tail = jnp.exp(cum_last - cum)
        kf = k_ref[bi]
        bfl = b_ref[bi]
        pre.append(dict(
            r_t=bf(r_ref[bi] * jnp.exp(cum)), kk_t=bf(kk_ref[bi] * jnp.exp(cum - lw)),
            k_t=bf(kf * g_inv), b_t=bf(bfl * g_inv), k_h=bf(kf * g_tail), b_h=bf(bfl * g_tail),
            v=bf(v_ref[bi]), g_last=jnp.exp(cum_last)))

    lhs_kr, Ls, Akks, RKs, Vbd = [], [], [], [], []
    for bi, g in units:
        p = pre[bi]
        lhs = jnp.concatenate([p["kk_t"][:, gsl(g)], p["r_t"][:, gsl(g)]], axis=0)
        ab = _dot_nt(lhs, block_diag(p["b_t"][:, gsl(g)]))
        ak = _dot_nt(lhs, block_diag(p["k_t"][:, gsl(g)]))
        lhs_kr.append(lhs)
        Ls.append(jnp.where(strict, ab[0:C], 0.0))
        Akks.append(bf(jnp.where(strict, ak[0:C], 0.0)))
        RKs.append(jnp.concatenate([bf(jnp.where(incl, ab[C:2 * C], 0.0)),
                                    bf(jnp.where(incl, ak[C:2 * C], 0.0))], axis=1))
        Vbd.append(block_diag(p["v"][:, gsl(g)]))
    AkkVs = [_dot(a, vb) for a, vb in zip(Akks, Vbd)]

    Xs = [eye - L for L in Ls]
    Qs = [_dot(bf(L), block_diag(bf(L))) for L in Ls]
    span = 4
    while span < C:
        nxt = [_dot(jnp.concatenate([bf(Q), bf(X)], axis=0), block_diag(bf(Q))) for Q, X in zip(Qs, Xs)]
        Qs = [n[0:C] for n in nxt]
        Xs = [X + n[C:2 * C] for X, n in zip(Xs, nxt)]
        span *= 2
    Xs = [X + _dot(bf(X), block_diag(bf(Q))) for Q, X in zip(Qs, Xs)]

    KRS = [_dot_nt(lhs, bf(state[bi, g])) for lhs, (bi, g) in zip(lhs_kr, units)]
    Es = [-_dot(bf(X), block_diag(bf(krs[0:C] + akkv))) for X, krs, akkv in zip(Xs, KRS, AkkVs)]
    ys = [[None] * n_groups for _ in range(n_batch)]
    for i, (bi, g) in enumerate(units):
        p = pre[bi]
        e16 = bf(Es[i])
        ys[bi][g] = KRS[i][C:2 * C] + _dot(RKs[i], jnp.concatenate([block_diag(e16), Vbd[i]], axis=0))
        upd = _dot_tn(jnp.concatenate([e16, p["v"][:, gsl(g)]], axis=0),
                      jnp.concatenate([p["b_h"][:, gsl(g)], p["k_h"][:, gsl(g)]], axis=0))
        state[bi, g] = state[bi, g] * p["g_last"][:, gsl(g)] + jnp.where(bd_mask, upd, 0.0)

    for bi in range(n_batch):
        y = jnp.concatenate(ys[bi], axis=-1)
        mean = _head_sum(y, N) * (1.0 / N)
        d = y - mean
        var = _head_sum(d * d, N) * (1.0 / N)
        yn = d * lax.rsqrt(var + RWKV_GN_EPS) * lnw_ref[...] + lnb_ref[...] + bonus_ref[bi]
        o_ref[bi] = (yn * g_ref[bi]).astype(o_ref.dtype)


def _rwkv_scan(r, k, v, kk, b, lw, bonus, g, ln_w, ln_b, *, B, S):
    M, D = r.shape
    C = RWKV_CHUNK
    assert S % C == 0 and C == RWKV_HEAD_DIM and D % (RWKV_GROUP_HEADS * RWKV_HEAD_DIM) == 0
    nc = S // C
    GW = RWKV_GROUP_HEADS * RWKV_HEAD_DIM
    body = functools.partial(_rwkv_scan_body, C=C, D=D)
    tile = pl.BlockSpec((B, C, D), lambda c: (0, c, 0))
    vec = pl.BlockSpec((1, D), lambda c: (0, 0))
    seq = lambda a: a.reshape(B, S, D)
    return pl.pallas_call(
        body,
        grid=(nc,),
        in_specs=[tile] * 8 + [vec, vec],
        out_specs=tile,
        out_shape=jax.ShapeDtypeStruct((B, S, D), BF16),
        scratch_shapes=[pltpu.VMEM((B, D // GW, GW, GW), F32)],
        compiler_params=_cparams("arbitrary"),
        name="rwkv_scan",
    )(*(seq(a) for a in (r, k, v, kk, b, lw, bonus, g)), ln_w.reshape(1, D), ln_b.reshape(1, D)).reshape(M, D)


def _rmsnorm_body(x_ref, g_ref, o_ref):
    o_ref[...] = _rms_rows(x_ref[...], g_ref[...])


def _rmsnorm(x2d, gain_row, *, tm=512):
    M, K = x2d.shape
    tm = min(tm, M)
    return pl.pallas_call(
        _rmsnorm_body,
        grid=(M // tm,),
        in_specs=[pl.BlockSpec((tm, K), lambda m: (m, 0)), pl.BlockSpec((1, K), lambda m: (0, 0))],
        out_specs=pl.BlockSpec((tm, K), lambda m: (m, 0)),
        out_shape=jax.ShapeDtypeStruct((M, K), F32),
        compiler_params=_cparams("parallel"),
        name="final_rmsnorm",
    )(x2d, gain_row)


def _cast_body(w_ref, o_ref):
    o_ref[...] = w_ref[...].astype(o_ref.dtype)


def _to_bf16(w, layer, cols=None):
    _, R, C = w.shape
    cols = cols or C
    tr = R
    while tr * cols * 4 > CAST_BLOCK_BYTES and tr % 32 == 0:
        tr //= 2
    return pl.pallas_call(
        _cast_body,
        grid=(R // tr,),
        in_specs=[pl.BlockSpec((None, tr, cols), lambda r: (layer, r, 0))],
        out_specs=pl.BlockSpec((tr, cols), lambda r: (r, 0)),
        out_shape=jax.ShapeDtypeStruct((R, cols), BF16),
        compiler_params=_cparams("parallel"),
        name="weight_to_bf16",
    )(w)


def _pad_cols(a, n):
    return jnp.pad(a, ((0, 0), (0, n - a.shape[1])))


def _pad_rows(a, n):
    return jnp.pad(a, ((0, n - a.shape[0]), (0, 0)))


def _fox_layer(x2d, B, S, gain_row, w_in, w_forget, b_f, q_gain, k_gain, w_o):
    M, D = x2d.shape
    H = FOX_HEADS
    w_f = _pad_cols(w_forget, LANES).astype(BF16)
    b_row = _pad_cols(b_f.reshape(1, H), LANES)
    scale = FOX_HEAD_DIM ** -0.5
    gains = jnp.concatenate([jnp.tile(q_gain * (scale * LOG2E), H), jnp.tile(k_gain, H)]).reshape(1, 2 * D)
    tn = 1024
    proj = functools.partial(_proj, x2d, gain_row, [w_in], seq=S, out_dtype=BF16, tn=tn)
    qk = proj(epilogue="head_norm", extras=(gains,), w_col_offsets=[0], n_out=2 * D)
    v = proj(epilogue="scale", w_col_offsets=[2 * D // tn], n_out=D)
    gate = proj(epilogue="sigmoid", w_col_offsets=[3 * D // tn], n_out=D)
    c = _forget_cumsum(x2d.reshape(B, S, D), gain_row, w_f, b_row)
    tk = min(FOX_KEY_BLOCK, S)
    ck_rows = c[:, :, :H].transpose(0, 2, 1).reshape(B, H, S // tk, tk)
    o = _fox_attention(qk.reshape(B, S, 2 * D), v.reshape(B, S, D), gate.reshape(B, S, D), ck_rows, tk=tk)
    return _matmul_residual(o.reshape(M, D), w_o, x2d)


def _rwkv_layer(x2d, B, S, gain_row, w_in, mu, w0, decay_b, a0, iclr_b, gate_b, k_k, k_a, r_k,
                ln_w, ln_b, w_o):
    M, D = x2d.shape
    o3 = 3 * D
    segs = [(0, o3, o3), (o3, DECAY_LORA, LANES), (o3 + DECAY_LORA, ICLR_LORA, LANES),
            (o3 + DECAY_LORA + ICLR_LORA, GATE_LORA, GATE_LORA)]
    w_pad = jnp.concatenate([_pad_cols(w_in[:, s:s + n], p) for s, n, p in segs], axis=1)
    mu_pad = jnp.concatenate([_pad_cols(mu[None, s:s + n], p) for s, n, p in segs], axis=1)
    P = w_pad.shape[1]
    tn = 512
    Pp = -(-P // tn) * tn
    w_pad = _pad_cols(w_pad, Pp).astype(BF16)
    mu_pad = _pad_cols(mu_pad, Pp)
    proj = _proj(x2d, gain_row, [w_pad], seq=S, epilogue="lerp", out_dtype=F32, extras=(mu_pad,), tn=tn)
    r, k, v, kk, b, lw, bonus, g = _rwkv_prep(
        proj, w0, _pad_rows(decay_b, LANES).astype(BF16), a0, _pad_rows(iclr_b, LANES).astype(BF16),
        gate_b.astype(BF16), k_k, k_a, r_k.reshape(-1), D=D)
    y = _rwkv_scan(r, k, v, kk, b, lw, bonus, g, ln_w, ln_b, B=B, S=S)
    return _matmul_residual(y, w_o, x2d)


def _conformer_layer(x2d, B, S, gain_row, w_in, dw_w, dw_b, ln_g, ln_b, w_o):
    M, D = x2d.shape
    tn = 512
    z = _proj(x2d, gain_row, [w_in], seq=S, epilogue="glu", out_dtype=F32,
              w_col_offsets=[0, D // tn], tn=tn)
    t = _conformer_conv(z.reshape(B, S, D), dw_w, dw_b, ln_g, ln_b)
    return _matmul_residual(t.reshape(M, D), w_o, x2d)


def _cross_layer(x2d, B, S, gain_row, kv, w_q, w_o):
    M, D = x2d.shape
    scale = (D // MEM_HEADS) ** -0.5
    q = _proj(x2d, gain_row, [w_q], seq=S, epilogue="scale", out_dtype=BF16, scale=scale, tn=1024)
    o = _cross_attention(q.reshape(B, S, D), kv)
    return _matmul_residual(o.reshape(M, D), w_o, x2d)


def _ffn_layer(x2d, S, gain_row, w_up, dw_w, dw_b, w_down):
    d_ff = w_down.shape[0]
    tn = 512
    act = _proj(x2d, gain_row, [w_up], seq=S, epilogue="ffn", out_dtype=BF16,
                extras=(dw_w, dw_b.reshape(1, -1)), w_col_offsets=[0, d_ff // tn], tn=tn)
    return _matmul_residual(act, w_down, x2d)


def kernel(x, mem, mem_norm, norm_mix, norm_cross, norm_ffn, final_norm, fox_w_in, fox_b_f, fox_q_gain, fox_k_gain, fox_w_o, rwkv_w_in, rwkv_mu, rwkv_w0, rwkv_decay_b, rwkv_a0, rwkv_iclr_b, rwkv_gate_b, rwkv_k_k, rwkv_k_a, rwkv_r_k, rwkv_ln_w, rwkv_ln_b, rwkv_w_o, conv_w_in, conv_dw_w, conv_dw_b, conv_ln_g, conv_ln_b, conv_w_o, cross_w_q, cross_w_kv, cross_w_o, ffn_w_up, ffn_dw_w, ffn_dw_b, ffn_w_down):
    B, S, D = x.shape
    depth = norm_mix.shape[0]
    Mem = mem.shape[1]
    M = B * S
    x2d = x.reshape(M, D)

    mem2d = mem.reshape(B * Mem, D)
    mem_gain = mem_norm.reshape(1, D)

    for i in range(depth):
        kind, j = i % 3, i // 3
        g_mix = norm_mix[i].reshape(1, D)
        if kind == 0:
            x2d = _fox_layer(x2d, B, S, g_mix, _to_bf16(fox_w_in, j, cols=4 * D), fox_w_in[j][:, 4 * D:],
                             fox_b_f[j], fox_q_gain[j], fox_k_gain[j], _to_bf16(fox_w_o, j))
        elif kind == 1:
            x2d = _rwkv_layer(x2d, B, S, g_mix, rwkv_w_in[j], rwkv_mu[j], rwkv_w0[j], rwkv_decay_b[j],
                              rwkv_a0[j], rwkv_iclr_b[j], rwkv_gate_b[j], rwkv_k_k[j], rwkv_k_a[j],
                              rwkv_r_k[j], rwkv_ln_w[j], rwkv_ln_b[j], _to_bf16(rwkv_w_o, j))
        else:
            x2d = _conformer_layer(x2d, B, S, g_mix, _to_bf16(conv_w_in, j), conv_dw_w[j], conv_dw_b[j],
                                   conv_ln_g[j], conv_ln_b[j], _to_bf16(conv_w_o, j))
        kv = _proj(mem2d, mem_gain, [_to_bf16(cross_w_kv, i)], seq=Mem, epilogue="scale", out_dtype=BF16,
                   tn=1024).reshape(B, Mem, 2 * D)
        x2d = _cross_layer(x2d, B, S, norm_cross[i].reshape(1, D), kv, _to_bf16(cross_w_q, i),
                           _to_bf16(cross_w_o, i))
        x2d = _ffn_layer(x2d, S, norm_ffn[i].reshape(1, D), _to_bf16(ffn_w_up, i), ffn_dw_w[i], ffn_dw_b[i],
                         _to_bf16(ffn_w_down, i))
    return _rmsnorm(x2d, final_norm.reshape(1, D)).reshape(B, S, D)
```

```python
import functools

import jax
import jax.numpy as jnp
from jax import lax
from jax.experimental import pallas as pl
from jax.experimental.pallas import tpu as pltpu

F32 = jnp.float32
BF16 = jnp.bfloat16

RMS_EPS = 1e-6
LN_EPS = 1e-5
RWKV_GN_EPS = 64e-5

LANES = 128
HALO = 16
SUB_COLS = 256
VMEM_LIMIT = 56 * 1024 * 1024
CAST_BLOCK_BYTES = 4 * 1024 * 1024

FOX_HEADS = 16
FOX_HEAD_DIM = 128
FOX_KEY_BLOCK = 1024
MEM_HEADS = 4
RWKV_HEAD_DIM = 64
RWKV_CHUNK = 64
RWKV_GROUP_HEADS = 4
DECAY_LORA = 96
ICLR_LORA = 96
GATE_LORA = 256
CONV_HALO = 32
CONV_PHASES = 4
ROW_PHASES = 4
NEG_BIG = -1e30
LOG2E = 1.4426950408889634


def _cparams(*sem):
    return pltpu.CompilerParams(dimension_semantics=sem, vmem_limit_bytes=VMEM_LIMIT)


def _rms_rows(x, g):
    ms = jnp.mean(x * x, axis=-1, keepdims=True)
    return x * lax.rsqrt(ms + RMS_EPS) * g


def _sigmoid(x):
    return 0.5 * jnp.tanh(0.5 * x) + 0.5


def _dot(a, b):
    return jnp.dot(a, b, preferred_element_type=F32)


def _dot_nt(a, b):
    return lax.dot_general(a, b, (((1,), (1,)), ((), ())), preferred_element_type=F32)


def _dot_tn(a, b):
    return lax.dot_general(a, b, (((0,), (0,)), ((), ())), preferred_element_type=F32)


def _split3(x):
    hi = x.astype(BF16)
    r1 = x - hi.astype(F32)
    mid = r1.astype(BF16)
    lo = (r1 - mid.astype(F32)).astype(BF16)
    return hi, mid, lo


def _dot_exact_lhs(sel, x):
    hi, mid, lo = _split3(x)
    return _dot(sel, hi) + _dot(sel, mid) + _dot(sel, lo)


def _dot_exact_rhs(x, sel):
    hi, mid, lo = _split3(x)
    return _dot(hi, sel) + _dot(mid, sel) + _dot(lo, sel)


def _proj_body(*refs, n_w, halo, epilogue, tm, tn, tiles_per_seq, scale):
    it = iter(refs)
    x_ref = next(it)
    xp_ref = next(it) if halo else None
    g_ref = next(it)
    w_refs = [next(it) for _ in range(n_w)]
    if epilogue == "ffn":
        cw_refs = [next(it) for _ in range(n_w)]
        cb_refs = [next(it) for _ in range(n_w)]
    elif epilogue in ("lerp", "head_norm"):
        vec_ref = next(it)
    out_ref = next(it)
    h_scr = next(it)
    acc_scr = next(it)
    stage_scr = next(it) if epilogue == "ffn" else None

    m = pl.program_id(0)
    n = pl.program_id(1)
    off = HALO if halo else 0

    @pl.when(n == 0)
    def _():
        g = g_ref[...]
        h_scr[pl.ds(off, tm), :] = _rms_rows(x_ref[...], g).astype(BF16)
        if halo:
            hp = _rms_rows(xp_ref[...], g)
            first = (m % tiles_per_seq) == 0
            h_scr[pl.ds(0, HALO), :] = jnp.where(first, 0.0, hp).astype(BF16)

    sub = min(SUB_COLS, tn)
    col_slices = [slice(c * sub, (c + 1) * sub) for c in range(tn // sub)]

    def store(cs, val):
        out_ref[:, cs] = val.astype(out_ref.dtype)

    def acc_rows(wi, cs, shift=0):
        return acc_scr[wi, pl.ds(off - shift, tm), cs]

    def epilogue_tile(cs):
        if epilogue == "scale":
            store(cs, acc_rows(0, cs) * scale)
        elif epilogue == "sigmoid":
            store(cs, _sigmoid(acc_rows(0, cs)))
        elif epilogue == "glu":
            store(cs, acc_rows(0, cs) * _sigmoid(acc_rows(1, cs)))
        elif epilogue == "head_norm":
            for j in range(sub // FOX_HEAD_DIM):
                lo = cs.start + j * FOX_HEAD_DIM
                sl = slice(lo, lo + FOX_HEAD_DIM)
                store(sl, _rms_rows(acc_rows(0, sl), vec_ref[:, sl]))
        elif epilogue == "ffn":
            P = ROW_PHASES
            for s in range(cs.start // LANES, cs.stop // LANES):
                ls = slice(s * LANES, (s + 1) * LANES)
                for p in range(P):
                    ys = []
                    for wi, (cw_ref, cb_ref) in enumerate(zip(cw_refs, cb_refs)):
                        cw = cw_ref[:, ls]
                        taps = cw.shape[0]
                        y = cb_ref[:, ls]
                        for j in range(taps):
                            rows_j = acc_scr[wi, s, pl.ds(off + p - (taps - 1 - j), tm // P, stride=P), :]
                            y = y + cw[j:j + 1, :] * rows_j
                        ys.append(y)
                    u, g2 = ys
                    stage_scr[s, pl.ds(p, tm // P, stride=P), :] = g2 * (jnp.tanh(g2) + 1.0) * u
                out_ref[:, ls] = stage_scr[s].astype(out_ref.dtype)
        elif epilogue == "lerp":
            cur = acc_rows(0, cs)
            store(cs, cur + (acc_rows(0, cs, shift=1) - cur) * vec_ref[:, cs])
        else:
            raise ValueError(epilogue)

    h = h_scr[...]
    for cs in col_slices:
        for wi, w_ref in enumerate(w_refs):
            acc = _dot(h, w_ref[:, cs])
            if epilogue == "ffn":
                for j in range(sub // LANES):
                    acc_scr[wi, cs.start // LANES + j] = acc[:, j * LANES:(j + 1) * LANES]
            else:
                acc_scr[wi, :, cs] = acc
        epilogue_tile(cs)


def _proj(x2d, gain_row, ws, *, seq, epilogue, out_dtype, extras=(), scale=1.0, tm=1024, tn=512,
          w_col_offsets=None, n_out=None):
    M, K = x2d.shape
    n_w = len(ws) if w_col_offsets is None else len(w_col_offsets)
    halo = epilogue in ("ffn", "lerp")
    tm = min(tm, M, seq)
    assert M % tm == 0 and seq % tm == 0 and tm % HALO == 0
    if w_col_offsets is None:
        w_col_offsets = [0] * n_w
        n_out = ws[0].shape[1]
        w_list = list(ws)
    else:
        w_list = [ws[0]] * n_w
        n_out = n_out or ws[0].shape[1] // n_w
    tn = min(tn, n_out)
    assert n_out % tn == 0

    def cols(o):
        return lambda m, n: (0, n + o)

    in_specs = [pl.BlockSpec((tm, K), lambda m, n: (m, 0))]
    args = [x2d]
    if halo:
        r = tm // HALO
        in_specs.append(pl.BlockSpec((HALO, K), lambda m, n: (jnp.maximum(m * r - 1, 0), 0)))
        args.append(x2d)
    in_specs.append(pl.BlockSpec((1, K), lambda m, n: (0, 0)))
    args.append(gain_row)
    for w, o in zip(w_list, w_col_offsets):
        in_specs.append(pl.BlockSpec((K, tn), cols(o)))
        args.append(w)
    if epilogue == "ffn":
        cw, cb = extras
        taps = cw.shape[0]
        assert taps - 1 <= HALO
        for o in w_col_offsets:
            in_specs.append(pl.BlockSpec((taps, tn), cols(o)))
            args.append(cw)
        for o in w_col_offsets:
            in_specs.append(pl.BlockSpec((1, tn), cols(o)))
            args.append(cb)
    elif epilogue in ("lerp", "head_norm"):
        in_specs.append(pl.BlockSpec((1, tn), cols(0)))
        args.append(extras[0])

    rows = tm + (HALO if halo else 0)
    if epilogue == "ffn":
        assert tm % ROW_PHASES == 0 and tn % LANES == 0
        scratch = [pltpu.VMEM((rows, K), BF16), pltpu.VMEM((n_w, tn // LANES, rows, LANES), F32),
                   pltpu.VMEM((tn // LANES, tm, LANES), F32)]
    else:
        scratch = [pltpu.VMEM((rows, K), BF16), pltpu.VMEM((n_w, rows, tn), F32)]

    body = functools.partial(_proj_body, n_w=n_w, halo=halo, epilogue=epilogue, tm=tm, tn=tn,
                             tiles_per_seq=seq // tm, scale=scale)
    return pl.pallas_call(
        body,
        grid=(M // tm, n_out // tn),
        in_specs=in_specs,
        out_specs=pl.BlockSpec((tm, tn), lambda m, n: (m, n)),
        out_shape=jax.ShapeDtypeStruct((M, n_out), out_dtype),
        scratch_shapes=scratch,
        compiler_params=_cparams("parallel", "arbitrary"),
        name="proj_" + epilogue,
    )(*args)


def _mm_res_body(a_ref, w_ref, r_ref, o_ref):
    a = a_ref[...]
    sub = min(SUB_COLS, o_ref.shape[-1])
    for c in range(o_ref.shape[-1] // sub):
        cs = slice(c * sub, (c + 1) * sub)
        o_ref[:, cs] = r_ref[:, cs] + _dot(a, w_ref[:, cs])


def _matmul_residual(a, w, res):
    M, K = a.shape
    N = w.shape[1]
    tm, tn = (512, 2048) if K <= 2048 else (1024, 512)
    tm, tn = min(tm, M), min(tn, N)
    assert M % tm == 0 and N % tn == 0
    return pl.pallas_call(
        _mm_res_body,
        grid=(M // tm, N // tn),
        in_specs=[pl.BlockSpec((tm, K), lambda m, n: (m, 0)),
                  pl.BlockSpec((K, tn), lambda m, n: (0, n)),
                  pl.BlockSpec((tm, tn), lambda m, n: (m, n))],
        out_specs=pl.BlockSpec((tm, tn), lambda m, n: (m, n)),
        out_shape=jax.ShapeDtypeStruct((M, N), F32),
        compiler_params=_cparams("parallel", "parallel"),
        name="matmul_residual",
    )(a, w, res)


def _forget_body(x_ref, g_ref, w_ref, b_ref, c_ref, carry):
    s = pl.program_id(1)

    @pl.when(s == 0)
    def _():
        carry[...] = jnp.zeros_like(carry)

    h = _rms_rows(x_ref[...], g_ref[...]).astype(BF16)
    z = _dot(h, w_ref[...]) + b_ref[...]
    logf = jnp.minimum(z, 0.0) - jnp.log(1.0 + jnp.exp(-jnp.abs(z)))
    tm = logf.shape[0]
    row = lax.broadcasted_iota(jnp.int32, (tm, tm), 0)
    col = lax.broadcasted_iota(jnp.int32, (tm, tm), 1)
    tri = (col <= row).astype(BF16)
    c = _dot_exact_lhs(tri, logf) + carry[...]
    c_ref[...] = c * LOG2E
    carry[...] = c[tm - 1:tm, :]


def _forget_cumsum(x, gain_row, w_f, b_f, *, tm=256):
    B, S, K = x.shape
    tm = min(tm, S)
    return pl.pallas_call(
        _forget_body,
        grid=(B, S // tm),
        in_specs=[pl.BlockSpec((None, tm, K), lambda b, s: (b, s, 0)),
                  pl.BlockSpec((1, K), lambda b, s: (0, 0)),
                  pl.BlockSpec((K, LANES), lambda b, s: (0, 0)),
                  pl.BlockSpec((1, LANES), lambda b, s: (0, 0))],
        out_specs=pl.BlockSpec((None, tm, LANES), lambda b, s: (b, s, 0)),
        out_shape=jax.ShapeDtypeStruct((B, S, LANES), F32),
        scratch_shapes=[pltpu.VMEM((1, LANES), F32)],
        compiler_params=_cparams("parallel", "arbitrary"),
        name="fox_forget_cumsum",
    )(x, gain_row, w_f, b_f)


def _fox_attn_body(q_ref, k_ref, v_ref, gate_ref, ck_ref, o_ref, m_scr, l_scr, acc_scr, *, tq, tk, heads):
    i = pl.program_id(2)
    hd = FOX_HEAD_DIM
    m_scr[...] = jnp.full_like(m_scr, NEG_BIG)
    l_scr[...] = jnp.zeros_like(l_scr)
    acc_scr[...] = jnp.zeros_like(acc_scr)

    def block(j, masked, lo=0, width=tk):
        start = pl.multiple_of(j * tk, tk) + lo
        if masked:
            qpos = i * tq + lax.broadcasted_iota(jnp.int32, (tq, width), 0)
            kpos = j * tk + lo + lax.broadcasted_iota(jnp.int32, (tq, width), 1)
            visible = kpos <= qpos
        for h in range(heads):
            sl = slice(h * hd, (h + 1) * hd)
            s = _dot_nt(q_ref[:, sl], k_ref[pl.ds(start, width), sl]) - ck_ref[h, pl.ds(j, 1), lo:lo + width]
            if masked:
                s = jnp.where(visible, s, NEG_BIG)
            m_old = m_scr[h]
            m_new = jnp.maximum(m_old, jnp.max(s, axis=-1, keepdims=True))
            alpha = jnp.exp2(m_old - m_new)
            p = jnp.exp2(s - jnp.tile(m_new, (1, width // LANES)))
            l_scr[h] = alpha * l_scr[h] + jnp.sum(p, axis=-1, keepdims=True)
            acc_scr[h] = alpha * acc_scr[h] + _dot(p.astype(BF16), v_ref[pl.ds(start, width), sl])
            m_scr[h] = m_new

    n_full = (i * tq) // tk

    def pair_step(jj, c):
        block(2 * jj, False)
        block(2 * jj + 1, False)
        return c

    lax.fori_loop(0, n_full // 2, pair_step, 0)

    @pl.when(n_full % 2 == 1)
    def _():
        block(n_full - 1, False)

    if tk == 2 * tq:
        pl.when(i % 2 == 0)(lambda: block(n_full, True, 0, tq))

        @pl.when(i % 2 == 1)
        def _():
            block(n_full, False, 0, tq)
            block(n_full, True, tq, tq)
    else:
        block(n_full, True)
    for h in range(heads):
        sl = slice(h * hd, (h + 1) * hd)
        o = acc_scr[h] / l_scr[h]
        o_ref[:, sl] = (o * gate_ref[:, sl].astype(F32)).astype(o_ref.dtype)


def _fox_attention(qk, v, gate, ck_rows, *, tq=512, tk=FOX_KEY_BLOCK, heads=2):
    B, S, D = v.shape
    H = D // FOX_HEAD_DIM
    HG = H // heads
    W = heads * FOX_HEAD_DIM
    tk = min(tk, S)
    tq = min(tq, tk)
    assert tk in (tq, 2 * tq) and S % tk == 0 and H % heads == 0
    body = functools.partial(_fox_attn_body, tq=tq, tk=tk, heads=heads)
    return pl.pallas_call(
        body,
        grid=(B, HG, S // tq),
        in_specs=[pl.BlockSpec((None, tq, W), lambda b, h, i: (b, i, h)),
                  pl.BlockSpec((None, S, W), lambda b, h, i: (b, 0, HG + h)),
                  pl.BlockSpec((None, S, W), lambda b, h, i: (b, 0, h)),
                  pl.BlockSpec((None, tq, W), lambda b, h, i: (b, i, h)),
                  pl.BlockSpec((None, heads, S // tk, tk), lambda b, h, i: (b, h, 0, 0))],
        out_specs=pl.BlockSpec((None, tq, W), lambda b, h, i: (b, i, h)),
        out_shape=jax.ShapeDtypeStruct((B, S, D), BF16),
        scratch_shapes=[pltpu.VMEM((heads, tq, LANES), F32), pltpu.VMEM((heads, tq, LANES), F32),
                        pltpu.VMEM((heads, tq, FOX_HEAD_DIM), F32)],
        compiler_params=_cparams("parallel", "parallel", "arbitrary"),
        name="fox_attention",
    )(qk, qk, v, gate, ck_rows)


def _cross_attn_body(q_ref, k_ref, v_ref, o_ref, *, heads):
    dh = q_ref.shape[-1] // heads
    for h in range(heads):
        sl = slice(h * dh, (h + 1) * dh)
        s = _dot_nt(q_ref[:, sl], k_ref[:, sl])
        s = s - jnp.max(s, axis=-1, keepdims=True)
        p = jnp.exp(s)
        l = jnp.sum(p, axis=-1, keepdims=True)
        o = _dot(p.astype(BF16), v_ref[:, sl])
        o_ref[:, sl] = (o / l).astype(o_ref.dtype)


def _cross_attention(q, kv, *, tq=512):
    B, S, D = q.shape
    Mem = kv.shape[1]
    tq = min(tq, S)
    body = functools.partial(_cross_attn_body, heads=MEM_HEADS)
    return pl.pallas_call(
        body,
        grid=(B, S // tq),
        in_specs=[pl.BlockSpec((None, tq, D), lambda b, i: (b, i, 0)),
                  pl.BlockSpec((None, Mem, D), lambda b, i: (b, 0, 0)),
                  pl.BlockSpec((None, Mem, D), lambda b, i: (b, 0, 1))],
        out_specs=pl.BlockSpec((None, tq, D), lambda b, i: (b, i, 0)),
        out_shape=jax.ShapeDtypeStruct((B, S, D), BF16),
        compiler_params=_cparams("parallel", "parallel"),
        name="cross_attention",
    )(q, kv, kv)


def _conformer_body(z_ref, zp_ref, w_ref, b_ref, g_ref, beta_ref, o_ref, zs, ys, *, tm, width):
    i = pl.program_id(1)
    D = z_ref.shape[-1]
    n_slabs = D // LANES
    P = CONV_PHASES
    base = CONV_HALO - (width - 1)
    first = i == 0
    for c in range(n_slabs):
        sl = slice(c * LANES, (c + 1) * LANES)
        zs[c, pl.ds(CONV_HALO, tm), :] = z_ref[:, sl]
        zs[c, pl.ds(0, CONV_HALO), :] = jnp.where(first, 0.0, zp_ref[:, sl])
    for c in range(n_slabs):
        sl = slice(c * LANES, (c + 1) * LANES)
        for p in range(P):
            acc = jnp.broadcast_to(b_ref[:, sl], (tm // P, LANES))
            for k in range(width):
                acc = acc + w_ref[k:k + 1, sl] * zs[c, pl.ds(base + k + p, tm // P, stride=P), :]
            ys[c, pl.ds(p, tm // P, stride=P), :] = acc
    total = ys[0]
    for c in range(1, n_slabs):
        total = total + ys[c]
    mu = jnp.sum(total, axis=-1, keepdims=True) * (1.0 / D)
    sq = jnp.zeros((tm, LANES), F32)
    for c in range(n_slabs):
        d = ys[c] - mu
        sq = sq + d * d
    inv = lax.rsqrt(jnp.sum(sq, axis=-1, keepdims=True) * (1.0 / D) + LN_EPS)
    for c in range(n_slabs):
        sl = slice(c * LANES, (c + 1) * LANES)
        t = (ys[c] - mu) * inv * g_ref[:, sl] + beta_ref[:, sl]
        o_ref[:, sl] = (t * _sigmoid(t)).astype(o_ref.dtype)


def _conformer_conv(z, dw_w, dw_b, ln_g, ln_b, *, tm=256):
    B, S, D = z.shape
    width = dw_w.shape[0]
    tm = min(tm, S)
    assert width - 1 <= CONV_HALO and tm % CONV_HALO == 0 and S % tm == 0
    r = tm // CONV_HALO
    body = functools.partial(_conformer_body, tm=tm, width=width)
    row = lambda a: a.reshape(1, D)
    return pl.pallas_call(
        body,
        grid=(B, S // tm),
        in_specs=[pl.BlockSpec((None, tm, D), lambda b, i: (b, i, 0)),
                  pl.BlockSpec((None, CONV_HALO, D), lambda b, i: (b, jnp.maximum(i * r - 1, 0), 0)),
                  pl.BlockSpec((width, D), lambda b, i: (0, 0)),
                  pl.BlockSpec((1, D), lambda b, i: (0, 0)),
                  pl.BlockSpec((1, D), lambda b, i: (0, 0)),
                  pl.BlockSpec((1, D), lambda b, i: (0, 0))],
        out_specs=pl.BlockSpec((None, tm, D), lambda b, i: (b, i, 0)),
        out_shape=jax.ShapeDtypeStruct((B, S, D), BF16),
        scratch_shapes=[pltpu.VMEM((D // LANES, tm + CONV_HALO, LANES), F32),
                        pltpu.VMEM((D // LANES, tm, LANES), F32)],
        compiler_params=_cparams("parallel", "parallel"),
        name="conformer_conv_ln",
    )(z, z, dw_w, row(dw_b), row(ln_g), row(ln_b))


def _head_sum(x, head_dim):
    n = x.shape[-1]
    r = lax.broadcasted_iota(jnp.int32, (LANES, LANES), 0) // head_dim
    c = lax.broadcasted_iota(jnp.int32, (LANES, LANES), 1) // head_dim
    ones = (r == c).astype(BF16)
    parts = []
    for j in range(n // LANES):
        parts.append(_dot_exact_rhs(x[:, j * LANES:(j + 1) * LANES], ones))
    return parts[0] if len(parts) == 1 else jnp.concatenate(parts, axis=-1)


def _rwkv_prep_body(p_ref, w0_ref, db_ref, a0_ref, ib_ref, gb_ref, kk_ref, ka_ref, rk_ref,
                    r_o, k_o, v_o, kk_o, b_o, lw_o, bonus_o, g_o, *, D):
    lw_pad = LANES
    r = p_ref[:, 0:D]
    k = p_ref[:, D:2 * D]
    v = p_ref[:, 2 * D:3 * D]
    o = 3 * D
    w_lo = p_ref[:, o:o + lw_pad]
    a_lo = p_ref[:, o + lw_pad:o + 2 * lw_pad]
    g_lo = p_ref[:, o + 2 * lw_pad:o + 2 * lw_pad + GATE_LORA]

    w_arg = w0_ref[...] + _dot(jnp.tanh(w_lo).astype(BF16), db_ref[...])
    softplus = jnp.maximum(-w_arg, 0.0) + jnp.log(1.0 + jnp.exp(-jnp.abs(w_arg)))
    lw_o[...] = -jnp.exp(-softplus - 0.5)
    a = _sigmoid(a0_ref[...] + _dot(a_lo.astype(BF16), ib_ref[...]))
    g_o[...] = _dot(_sigmoid(g_lo).astype(BF16), gb_ref[...])
    kk = k * kk_ref[...]
    nrm = jnp.sqrt(_head_sum(kk * kk, RWKV_HEAD_DIM))
    kk = kk / jnp.maximum(nrm, 1e-12)
    k2 = k * (1.0 + (a - 1.0) * ka_ref[...])
    r_o[...] = r
    k_o[...] = k2
    v_o[...] = v
    kk_o[...] = kk
    b_o[...] = kk * a
    bonus_o[...] = _head_sum(r * k2 * rk_ref[...], RWKV_HEAD_DIM) * v


def _rwkv_prep(proj, w0, decay_b, a0, iclr_b, gate_b, k_k, k_a, r_k, *, D, tm=256):
    M, P = proj.shape
    tm = min(tm, M)
    body = functools.partial(_rwkv_prep_body, D=D)
    row = lambda a: a.reshape(1, D)
    full = lambda a: pl.BlockSpec(a.shape, lambda m: (0, 0))
    ins = [proj, row(w0), decay_b, row(a0), iclr_b, gate_b, row(k_k), row(k_a), row(r_k)]
    out_spec = pl.BlockSpec((tm, D), lambda m: (m, 0))
    return pl.pallas_call(
        body,
        grid=(M // tm,),
        in_specs=[pl.BlockSpec((tm, P), lambda m: (m, 0))] + [full(a) for a in ins[1:]],
        out_specs=[out_spec] * 8,
        out_shape=[jax.ShapeDtypeStruct((M, D), F32)] * 8,
        compiler_params=_cparams("parallel"),
        name="rwkv_prep",
    )(*ins)


def _rwkv_scan_body(r_ref, k_ref, v_ref, kk_ref, b_ref, lw_ref, bonus_ref, g_ref, lnw_ref, lnb_ref,
                    o_ref, state, *, C, D):
    c_idx = pl.program_id(0)

    @pl.when(c_idx == 0)
    def _():
        state[...] = jnp.zeros_like(state)

    N = RWKV_HEAD_DIM
    GH = RWKV_GROUP_HEADS
    GW = GH * N
    n_groups = D // GW
    n_batch = r_ref.shape[0]
    bf = lambda a: a.astype(BF16)

    row = lax.broadcasted_iota(jnp.int32, (C, GW), 0)
    lane = lax.broadcasted_iota(jnp.int32, (C, GW), 1)
    strict = (lane % N) < row
    incl = (lane % N) <= row
    eye = ((lane % N) == row).astype(F32)
    head_of_lane = lax.broadcasted_iota(jnp.int32, (1, GW), 1) // N
    bd_mask = (lax.broadcasted_iota(jnp.int32, (GW, GW), 0) // N) == (lax.broadcasted_iota(jnp.int32, (GW, GW), 1) // N)
    tri = (lax.broadcasted_iota(jnp.int32, (C, C), 1) <= lax.broadcasted_iota(jnp.int32, (C, C), 0)).astype(BF16)

    def block_diag(x):
        return jnp.concatenate([jnp.where(head_of_lane == h, x, jnp.zeros_like(x)) for h in range(GH)], axis=0)

    units = [(bi, g) for bi in range(n_batch) for g in range(n_groups)]
    gsl = lambda g: slice(g * GW, (g + 1) * GW)

    pre = []
    for bi in range(n_batch):
        lw = lw_ref[bi]
        cum = _dot_exact_lhs(tri, lw)
        cum_last = cum[C - 1:C, :]
        g_inv = jnp.exp(-cum)
        g_tail = jnp.exp(cum_last - cum)
        kf = k_ref[bi]
        bfl = b_ref[bi]
        pre.append(dict(
            r_t=bf(r_ref[bi] * jnp.exp(cum)), kk_t=bf(kk_ref[bi] * jnp.exp(cum - lw)),
            k_t=bf(kf * g_inv), b_t=bf(bfl * g_inv), k_h=bf(kf * g_tail), b_h=bf(bfl * g_tail),
            v=bf(v_ref[bi]), g_last=jnp.exp(cum_last)))

    lhs_kr, Ls, Akks, RKs, Vbd = [], [], [], [], []
    for bi, g in units:
        p = pre[bi]
        lhs = jnp.concatenate([p["kk_t"][:, gsl(g)], p["r_t"][:, gsl(g)]], axis=0)
        ab = _dot_nt(lhs, block_diag(p["b_t"][:, gsl(g)]))
        ak = _dot_nt(lhs, block_diag(p["k_t"][:, gsl(g)]))
        lhs_kr.append(lhs)
        Ls.append(jnp.where(strict, ab[0:C], 0.0))
        Akks.append(bf(jnp.where(strict, ak[0:C], 0.0)))
        RKs.append(jnp.concatenate([bf(jnp.where(incl, ab[C:2 * C], 0.0)),
                                    bf(jnp.where(incl, ak[C:2 * C], 0.0))], axis=1))
        Vbd.append(block_diag(p["v"][:, gsl(g)]))
    AkkVs = [_dot(a, vb) for a, vb in zip(Akks, Vbd)]

    Xs = [eye - L for L in Ls]
    Qs = [_dot(bf(L), block_diag(bf(L))) for L in Ls]
    span = 4
    while span < C:
        nxt = [_dot(jnp.concatenate([bf(Q), bf(X)], axis=0), block_diag(bf(Q))) for Q, X in zip(Qs, Xs)]
        Qs = [n[0:C] for n in nxt]
        Xs = [X + n[C:2 * C] for X, n in zip(Xs, nxt)]
        span *= 2
    Xs = [X + _dot(bf(X), block_diag(bf(Q))) for Q, X in zip(Qs, Xs)]

    KRS = [_dot_nt(lhs, bf(state[bi, g])) for lhs, (bi, g) in zip(lhs_kr, units)]
    Es = [-_dot(bf(X), block_diag(bf(krs[0:C] + akkv))) for X, krs, akkv in zip(Xs, KRS, AkkVs)]
    ys = [[None] * n_groups for _ in range(n_batch)]
    for i, (bi, g) in enumerate(units):
        p = pre[bi]
        e16 = bf(Es[i])
        ys[bi][g] = KRS[i][C:2 * C] + _dot(RKs[i], jnp.concatenate([block_diag(e16), Vbd[i]], axis=0))
        upd = _dot_tn(jnp.concatenate([e16, p["v"][:, gsl(g)]], axis=0),
                      jnp.concatenate([p["b_h"][:, gsl(g)], p["k_h"][:, gsl(g)]], axis=0))
        state[bi, g] = state[bi, g] * p["g_last"][:, gsl(g)] + jnp.where(bd_mask, upd, 0.0)

    for bi in range(n_batch):
        y = jnp.concatenate(ys[bi], axis=-1)
        mean = _head_sum(y, N) * (1.0 / N)
        d = y - mean
        var = _head_sum(d * d, N) * (1.0 / N)
        yn = d * lax.rsqrt(var + RWKV_GN_EPS) * lnw_ref[...] + lnb_ref[...] + bonus_ref[bi]
        o_ref[bi] = (yn * g_ref[bi]).astype(o_ref.dtype)


def _rwkv_scan(r, k, v, kk, b, lw, bonus, g, ln_w, ln_b, *, B, S):
    M, D = r.shape
    C = RWKV_CHUNK
    assert S % C == 0 and C == RWKV_HEAD_DIM and D % (RWKV_GROUP_HEADS * RWKV_HEAD_DIM) == 0
    nc = S // C
    GW = RWKV_GROUP_HEADS * RWKV_HEAD_DIM
    body = functools.partial(_rwkv_scan_body, C=C, D=D)
    tile = pl.BlockSpec((B, C, D), lambda c: (0, c, 0))
    vec = pl.BlockSpec((1, D), lambda c: (0, 0))
    seq = lambda a: a.reshape(B, S, D)
    return pl.pallas_call(
        body,
        grid=(nc,),
        in_specs=[tile] * 8 + [vec, vec],
        out_specs=tile,
        out_shape=jax.ShapeDtypeStruct((B, S, D), BF16),
        scratch_shapes=[pltpu.VMEM((B, D // GW, GW, GW), F32)],
        compiler_params=_cparams("arbitrary"),
        name="rwkv_scan",
    )(*(seq(a) for a in (r, k, v, kk, b, lw, bonus, g)), ln_w.reshape(1, D), ln_b.reshape(1, D)).reshape(M, D)


def _rmsnorm_body(x_ref, g_ref, o_ref):
    o_ref[...] = _rms_rows(x_ref[...], g_ref[...])


def _rmsnorm(x2d, gain_row, *, tm=512):
    M, K = x2d.shape
    tm = min(tm, M)
    return pl.pallas_call(
        _rmsnorm_body,
        grid=(M // tm,),
        in_specs=[pl.BlockSpec((tm, K), lambda m: (m, 0)), pl.BlockSpec((1, K), lambda m: (0, 0))],
        out_specs=pl.BlockSpec((tm, K), lambda m: (m, 0)),
        out_shape=jax.ShapeDtypeStruct((M, K), F32),
        compiler_params=_cparams("parallel"),
        name="final_rmsnorm",
    )(x2d, gain_row)


def _cast_body(w_ref, o_ref):
    o_ref[...] = w_ref[...].astype(o_ref.dtype)


def _to_bf16(w, layer, cols=None):
    _, R, C = w.shape
    cols = cols or C
    tr = R
    while tr * cols * 4 > CAST_BLOCK_BYTES and tr % 32 == 0:
        tr //= 2
    return pl.pallas_call(
        _cast_body,
        grid=(R // tr,),
        in_specs=[pl.BlockSpec((None, tr, cols), lambda r: (layer, r, 0))],
        out_specs=pl.BlockSpec((tr, cols), lambda r: (r, 0)),
        out_shape=jax.ShapeDtypeStruct((R, cols), BF16),
        compiler_params=_cparams("parallel"),
        name="weight_to_bf16",
    )(w)


def _pad_cols(a, n):
    return jnp.pad(a, ((0, 0), (0, n - a.shape[1])))


def _pad_rows(a, n):
    return jnp.pad(a, ((0, n - a.shape[0]), (0, 0)))


def _fox_layer(x2d, B, S, gain_row, w_in, w_forget, b_f, q_gain, k_gain, w_o):
    M, D = x2d.shape
    H = FOX_HEADS
    w_f = _pad_cols(w_forget, LANES).astype(BF16)
    b_row = _pad_cols(b_f.reshape(1, H), LANES)
    scale = FOX_HEAD_DIM ** -0.5
    gains = jnp.concatenate([jnp.tile(q_gain * (scale * LOG2E), H), jnp.tile(k_gain, H)]).reshape(1, 2 * D)
    tn = 1024
    proj = functools.partial(_proj, x2d, gain_row, [w_in], seq=S, out_dtype=BF16, tn=tn)
    qk = proj(epilogue="head_norm", extras=(gains,), w_col_offsets=[0], n_out=2 * D)
    v = proj(epilogue="scale", w_col_offsets=[2 * D // tn], n_out=D)
    gate = proj(epilogue="sigmoid", w_col_offsets=[3 * D // tn], n_out=D)
    c = _forget_cumsum(x2d.reshape(B, S, D), gain_row, w_f, b_row)
    tk = min(FOX_KEY_BLOCK, S)
    ck_rows = c[:, :, :H].transpose(0, 2, 1).reshape(B, H, S // tk, tk)
    o = _fox_attention(qk.reshape(B, S, 2 * D), v.reshape(B, S, D), gate.reshape(B, S, D), ck_rows, tk=tk)
    return _matmul_residual(o.reshape(M, D), w_o, x2d)


def _rwkv_layer(x2d, B, S, gain_row, w_in, mu, w0, decay_b, a0, iclr_b, gate_b, k_k, k_a, r_k,
                ln_w, ln_b, w_o):
    M, D = x2d.shape
    o3 = 3 * D
    segs = [(0, o3, o3), (o3, DECAY_LORA, LANES), (o3 + DECAY_LORA, ICLR_LORA, LANES),
            (o3 + DECAY_LORA + ICLR_LORA, GATE_LORA, GATE_LORA)]
    w_pad = jnp.concatenate([_pad_cols(w_in[:, s:s + n], p) for s, n, p in segs], axis=1)
    mu_pad = jnp.concatenate([_pad_cols(mu[None, s:s + n], p) for s, n, p in segs], axis=1)
    P = w_pad.shape[1]
    tn = 512
    Pp = -(-P // tn) * tn
    w_pad = _pad_cols(w_pad, Pp).astype(BF16)
    mu_pad = _pad_cols(mu_pad, Pp)
    proj = _proj(x2d, gain_row, [w_pad], seq=S, epilogue="lerp", out_dtype=F32, extras=(mu_pad,), tn=tn)
    r, k, v, kk, b, lw, bonus, g = _rwkv_prep(
        proj, w0, _pad_rows(decay_b, LANES).astype(BF16), a0, _pad_rows(iclr_b, LANES).astype(BF16),
        gate_b.astype(BF16), k_k, k_a, r_k.reshape(-1), D=D)
    y = _rwkv_scan(r, k, v, kk, b, lw, bonus, g, ln_w, ln_b, B=B, S=S)
    return _matmul_residual(y, w_o, x2d)


def _conformer_layer(x2d, B, S, gain_row, w_in, dw_w, dw_b, ln_g, ln_b, w_o):
    M, D = x2d.shape
    tn = 512
    z = _proj(x2d, gain_row, [w_in], seq=S, epilogue="glu", out_dtype=F32,
              w_col_offsets=[0, D // tn], tn=tn)
    t = _conformer_conv(z.reshape(B, S, D), dw_w, dw_b, ln_g, ln_b)
    return _matmul_residual(t.reshape(M, D), w_o, x2d)


def _cross_layer(x2d, B, S, gain_row, kv, w_q, w_o):
    M, D = x2d.shape
    scale = (D // MEM_HEADS) ** -0.5
    q = _proj(x2d, gain_row, [w_q], seq=S, epilogue="scale", out_dtype=BF16, scale=scale, tn=1024)
    o = _cross_attention(q.reshape(B, S, D), kv)
    return _matmul_residual(o.reshape(M, D), w_o, x2d)


def _ffn_layer(x2d, S, gain_row, w_up, dw_w, dw_b, w_down):
    d_ff = w_down.shape[0]
    tn = 512
    half = jnp.concatenate([jnp.ones((d_ff,), F32), jnp.full((d_ff,), 0.5, F32)])
    act = _proj(x2d, gain_row, [w_up], seq=S, epilogue="ffn", out_dtype=BF16,
                extras=(dw_w * half, (dw_b * half).reshape(1, -1)), w_col_offsets=[0, d_ff // tn], tn=tn)
    return _matmul_residual(act, w_down, x2d)


def kernel(x, mem, mem_norm, norm_mix, norm_cross, norm_ffn, final_norm, fox_w_in, fox_b_f, fox_q_gain, fox_k_gain, fox_w_o, rwkv_w_in, rwkv_mu, rwkv_w0, rwkv_decay_b, rwkv_a0, rwkv_iclr_b, rwkv_gate_b, rwkv_k_k, rwkv_k_a, rwkv_r_k, rwkv_ln_w, rwkv_ln_b, rwkv_w_o, conv_w_in, conv_dw_w, conv_dw_b, conv_ln_g, conv_ln_b, conv_w_o, cross_w_q, cross_w_kv, cross_w_o, ffn_w_up, ffn_dw_w, ffn_dw_b, ffn_w_down):
    B, S, D = x.shape
    depth = norm_mix.shape[0]
    Mem = mem.shape[1]
    M = B * S
    x2d = x.reshape(M, D)

    mem2d = mem.reshape(B * Mem, D)
    mem_gain = mem_norm.reshape(1, D)

    for i in range(depth):
        kind, j = i % 3, i // 3
        g_mix = norm_mix[i].reshape(1, D)
        if kind == 0:
            x2d = _fox_layer(x2d, B, S, g_mix, fox_w_in[j][:, :4 * D].astype(BF16), fox_w_in[j][:, 4 * D:],
                             fox_b_f[j], fox_q_gain[j], fox_k_gain[j], _to_bf16(fox_w_o, j))
        elif kind == 1:
            x2d = _rwkv_layer(x2d, B, S, g_mix, rwkv_w_in[j], rwkv_mu[j], rwkv_w0[j], rwkv_decay_b[j],
                              rwkv_a0[j], rwkv_iclr_b[j], rwkv_gate_b[j], rwkv_k_k[j], rwkv_k_a[j],
                              rwkv_r_k[j], rwkv_ln_w[j], rwkv_ln_b[j], _to_bf16(rwkv_w_o, j))
        else:
            x2d = _conformer_layer(x2d, B, S, g_mix, _to_bf16(conv_w_in, j), conv_dw_w[j], conv_dw_b[j],
                                   conv_ln_g[j], conv_ln_b[j], _to_bf16(conv_w_o, j))
        kv = _proj(mem2d, mem_gain, [_to_bf16(cross_w_kv, i)], seq=Mem, epilogue="scale", out_dtype=BF16,
                   tn=1024).reshape(B, Mem, 2 * D)
        x2d = _cross_layer(x2d, B, S, norm_cross[i].reshape(1, D), kv, _to_bf16(cross_w_q, i),
                           _to_bf16(cross_w_o, i))
        x2d = _ffn_layer(x2d, S, norm_ffn[i].reshape(1, D), _to_bf16(ffn_w_up, i), ffn_dw_w[i], ffn_dw_b[i],
                         _to_bf16(ffn_w_down, i))
    return _rmsnorm(x2d, final_norm.reshape(1, D)).reshape(B, S, D)
```

```python
import functools

import jax
import jax.numpy as jnp
from jax import lax
from jax.experimental import pallas as pl
from jax.experimental.pallas import tpu as pltpu

F32 = jnp.float32
BF16 = jnp.bfloat16

RMS_EPS = 1e-6
LN_EPS = 1e-5
RWKV_GN_EPS = 64e-5

LANES = 128
HALO = 16
SUB_COLS = 256
VMEM_LIMIT = 56 * 1024 * 1024
CAST_BLOCK_BYTES = 4 * 1024 * 1024

FOX_HEADS = 16
FOX_HEAD_DIM = 128
FOX_KEY_BLOCK = 1024
MEM_HEADS = 4
RWKV_HEAD_DIM = 64
RWKV_CHUNK = 64
RWKV_GROUP_HEADS = 4
DECAY_LORA = 96
ICLR_LORA = 96
GATE_LORA = 256
CONV_HALO = 32
CONV_PHASES = 4
ROW_PHASES = 4
NEG_BIG = -1e30
LOG2E = 1.4426950408889634


def _cparams(*sem):
    return pltpu.CompilerParams(dimension_semantics=sem, vmem_limit_bytes=VMEM_LIMIT)


def _rms_rows(x, g):
    ms = jnp.mean(x * x, axis=-1, keepdims=True)
    return x * lax.rsqrt(ms + RMS_EPS) * g


def _sigmoid(x):
    return 0.5 * jnp.tanh(0.5 * x) + 0.5


def _dot(a, b):
    return jnp.dot(a, b, preferred_element_type=F32)


def _dot_nt(a, b):
    return lax.dot_general(a, b, (((1,), (1,)), ((), ())), preferred_element_type=F32)


def _dot_tn(a, b):
    return lax.dot_general(a, b, (((0,), (0,)), ((), ())), preferred_element_type=F32)


def _split3(x):
    hi = x.astype(BF16)
    r1 = x - hi.astype(F32)
    mid = r1.astype(BF16)
    lo = (r1 - mid.astype(F32)).astype(BF16)
    return hi, mid, lo


def _dot_exact_lhs(sel, x):
    hi, mid, lo = _split3(x)
    return _dot(sel, hi) + _dot(sel, mid) + _dot(sel, lo)


def _dot_hi_lo_rhs(x, sel):
    hi = x.astype(BF16)
    lo = (x - hi.astype(F32)).astype(BF16)
    return _dot(hi, sel) + _dot(lo, sel)


def _proj_body(*refs, n_w, halo, epilogue, tm, tn, tiles_per_seq, scale):
    it = iter(refs)
    x_ref = next(it)
    xp_ref = next(it) if halo else None
    g_ref = next(it)
    w_refs = [next(it) for _ in range(n_w)]
    if epilogue == "ffn":
        cw_refs = [next(it) for _ in range(n_w)]
        cb_refs = [next(it) for _ in range(n_w)]
    elif epilogue in ("lerp", "head_norm"):
        vec_ref = next(it)
    out_ref = next(it)
    h_scr = next(it)
    acc_scr = next(it)
    stage_scr = next(it) if epilogue == "ffn" else None

    m = pl.program_id(0)
    n = pl.program_id(1)
    off = HALO if halo else 0

    @pl.when(n == 0)
    def _():
        g = g_ref[...]
        h_scr[pl.ds(off, tm), :] = _rms_rows(x_ref[...], g).astype(BF16)
        if halo:
            hp = _rms_rows(xp_ref[...], g)
            first = (m % tiles_per_seq) == 0
            h_scr[pl.ds(0, HALO), :] = jnp.where(first, 0.0, hp).astype(BF16)

    sub = min(SUB_COLS, tn)
    col_slices = [slice(c * sub, (c + 1) * sub) for c in range(tn // sub)]

    def store(cs, val):
        out_ref[:, cs] = val.astype(out_ref.dtype)

    def acc_rows(wi, cs, shift=0):
        return acc_scr[wi, pl.ds(off - shift, tm), cs]

    def epilogue_tile(cs):
        if epilogue == "scale":
            store(cs, acc_rows(0, cs) * scale)
        elif epilogue == "sigmoid":
            store(cs, _sigmoid(acc_rows(0, cs)))
        elif epilogue == "glu":
            store(cs, acc_rows(0, cs) * _sigmoid(acc_rows(1, cs)))
        elif epilogue == "head_norm":
            for j in range(sub // FOX_HEAD_DIM):
                lo = cs.start + j * FOX_HEAD_DIM
                sl = slice(lo, lo + FOX_HEAD_DIM)
                store(sl, _rms_rows(acc_rows(0, sl), vec_ref[:, sl]))
        elif epilogue == "ffn":
            P = ROW_PHASES
            for s in range(cs.start // LANES, cs.stop // LANES):
                ls = slice(s * LANES, (s + 1) * LANES)
                for p in range(P):
                    ys = []
                    for wi, (cw_ref, cb_ref) in enumerate(zip(cw_refs, cb_refs)):
                        cw = cw_ref[:, ls]
                        taps = cw.shape[0]
                        y = cb_ref[:, ls]
                        for j in range(taps):
                            rows_j = acc_scr[wi, s, pl.ds(off + p - (taps - 1 - j), tm // P, stride=P), :]
                            y = y + cw[j:j + 1, :] * rows_j
                        ys.append(y)
                    u, g2 = ys
                    stage_scr[s, pl.ds(p, tm // P, stride=P), :] = g2 * (jnp.tanh(g2) + 1.0) * u
                out_ref[:, ls] = stage_scr[s].astype(out_ref.dtype)
        elif epilogue == "lerp":
            cur = acc_rows(0, cs)
            store(cs, cur + (acc_rows(0, cs, shift=1) - cur) * vec_ref[:, cs])
        else:
            raise ValueError(epilogue)

    h = h_scr[...]
    for cs in col_slices:
        for wi, w_ref in enumerate(w_refs):
            acc = _dot(h, w_ref[:, cs])
            if epilogue == "ffn":
                for j in range(sub // LANES):
                    acc_scr[wi, cs.start // LANES + j] = acc[:, j * LANES:(j + 1) * LANES]
            else:
                acc_scr[wi, :, cs] = acc
        epilogue_tile(cs)


def _proj(x2d, gain_row, ws, *, seq, epilogue, out_dtype, extras=(), scale=1.0, tm=1024, tn=512,
          w_col_offsets=None, n_out=None):
    M, K = x2d.shape
    n_w = len(ws) if w_col_offsets is None else len(w_col_offsets)
    halo = epilogue in ("ffn", "lerp")
    tm = min(tm, M, seq)
    assert M % tm == 0 and seq % tm == 0 and tm % HALO == 0
    if w_col_offsets is None:
        w_col_offsets = [0] * n_w
        n_out = ws[0].shape[1]
        w_list = list(ws)
    else:
        w_list = [ws[0]] * n_w
        n_out = n_out or ws[0].shape[1] // n_w
    tn = min(tn, n_out)
    assert n_out % tn == 0

    def cols(o):
        return lambda m, n: (0, n + o)

    in_specs = [pl.BlockSpec((tm, K), lambda m, n: (m, 0))]
    args = [x2d]
    if halo:
        r = tm // HALO
        in_specs.append(pl.BlockSpec((HALO, K), lambda m, n: (jnp.maximum(m * r - 1, 0), 0)))
        args.append(x2d)
    in_specs.append(pl.BlockSpec((1, K), lambda m, n: (0, 0)))
    args.append(gain_row)
    for w, o in zip(w_list, w_col_offsets):
        in_specs.append(pl.BlockSpec((K, tn), cols(o)))
        args.append(w)
    if epilogue == "ffn":
        cw, cb = extras
        taps = cw.shape[0]
        assert taps - 1 <= HALO
        for o in w_col_offsets:
            in_specs.append(pl.BlockSpec((taps, tn), cols(o)))
            args.append(cw)
        for o in w_col_offsets:
            in_specs.append(pl.BlockSpec((1, tn), cols(o)))
            args.append(cb)
    elif epilogue in ("lerp", "head_norm"):
        in_specs.append(pl.BlockSpec((1, tn), cols(0)))
        args.append(extras[0])

    rows = tm + (HALO if halo else 0)
    if epilogue == "ffn":
        assert tm % ROW_PHASES == 0 and tn % LANES == 0
        scratch = [pltpu.VMEM((rows, K), BF16), pltpu.VMEM((n_w, tn // LANES, rows, LANES), F32),
                   pltpu.VMEM((tn // LANES, tm, LANES), F32)]
    else:
        scratch = [pltpu.VMEM((rows, K), BF16), pltpu.VMEM((n_w, rows, tn), F32)]

    body = functools.partial(_proj_body, n_w=n_w, halo=halo, epilogue=epilogue, tm=tm, tn=tn,
                             tiles_per_seq=seq // tm, scale=scale)
    return pl.pallas_call(
        body,
        grid=(M // tm, n_out // tn),
        in_specs=in_specs,
        out_specs=pl.BlockSpec((tm, tn), lambda m, n: (m, n)),
        out_shape=jax.ShapeDtypeStruct((M, n_out), out_dtype),
        scratch_shapes=scratch,
        compiler_params=_cparams("parallel", "arbitrary"),
        name="proj_" + epilogue,
    )(*args)


def _mm_res_body(a_ref, w_ref, r_ref, o_ref):
    a = a_ref[...]
    sub = min(SUB_COLS, o_ref.shape[-1])
    for c in range(o_ref.shape[-1] // sub):
        cs = slice(c * sub, (c + 1) * sub)
        o_ref[:, cs] = r_ref[:, cs] + _dot(a, w_ref[:, cs])


def _matmul_residual(a, w, res):
    M, K = a.shape
    N = w.shape[1]
    tm, tn = (512, 2048) if K <= 2048 else (1024, 512)
    tm, tn = min(tm, M), min(tn, N)
    assert M % tm == 0 and N % tn == 0
    return pl.pallas_call(
        _mm_res_body,
        grid=(M // tm, N // tn),
        in_specs=[pl.BlockSpec((tm, K), lambda m, n: (m, 0)),
                  pl.BlockSpec((K, tn), lambda m, n: (0, n)),
                  pl.BlockSpec((tm, tn), lambda m, n: (m, n))],
        out_specs=pl.BlockSpec((tm, tn), lambda m, n: (m, n)),
        out_shape=jax.ShapeDtypeStruct((M, N), F32),
        compiler_params=_cparams("parallel", "parallel"),
        name="matmul_residual",
    )(a, w, res)


def _forget_body(x_ref, g_ref, w_ref, b_ref, c_ref, carry):
    s = pl.program_id(1)

    @pl.when(s == 0)
    def _():
        carry[...] = jnp.zeros_like(carry)

    h = _rms_rows(x_ref[...], g_ref[...]).astype(BF16)
    z = _dot(h, w_ref[...]) + b_ref[...]
    logf = jnp.minimum(z, 0.0) - jnp.log(1.0 + jnp.exp(-jnp.abs(z)))
    tm = logf.shape[0]
    row = lax.broadcasted_iota(jnp.int32, (tm, tm), 0)
    col = lax.broadcasted_iota(jnp.int32, (tm, tm), 1)
    tri = (col <= row).astype(BF16)
    c = _dot_exact_lhs(tri, logf) + carry[...]
    c_ref[...] = c * LOG2E
    carry[...] = c[tm - 1:tm, :]


def _forget_cumsum(x, gain_row, w_f, b_f, *, tm=256):
    B, S, K = x.shape
    tm = min(tm, S)
    return pl.pallas_call(
        _forget_body,
        grid=(B, S // tm),
        in_specs=[pl.BlockSpec((None, tm, K), lambda b, s: (b, s, 0)),
                  pl.BlockSpec((1, K), lambda b, s: (0, 0)),
                  pl.BlockSpec((K, LANES), lambda b, s: (0, 0)),
                  pl.BlockSpec((1, LANES), lambda b, s: (0, 0))],
        out_specs=pl.BlockSpec((None, tm, LANES), lambda b, s: (b, s, 0)),
        out_shape=jax.ShapeDtypeStruct((B, S, LANES), F32),
        scratch_shapes=[pltpu.VMEM((1, LANES), F32)],
        compiler_params=_cparams("parallel", "arbitrary"),
        name="fox_forget_cumsum",
    )(x, gain_row, w_f, b_f)


def _fox_attn_body(q_ref, k_ref, v_ref, gate_ref, ck_ref, o_ref, m_scr, l_scr, acc_scr, *, tq, tk, heads):
    i = pl.program_id(2)
    hd = FOX_HEAD_DIM
    m_scr[...] = jnp.full_like(m_scr, NEG_BIG)
    l_scr[...] = jnp.zeros_like(l_scr)
    acc_scr[...] = jnp.zeros_like(acc_scr)

    def block(j, masked, lo=0, width=tk):
        start = pl.multiple_of(j * tk, tk) + lo
        if masked:
            qpos = i * tq + lax.broadcasted_iota(jnp.int32, (tq, width), 0)
            kpos = j * tk + lo + lax.broadcasted_iota(jnp.int32, (tq, width), 1)
            visible = kpos <= qpos
        for h in range(heads):
            sl = slice(h * hd, (h + 1) * hd)
            s = _dot_nt(q_ref[:, sl], k_ref[pl.ds(start, width), sl]) - ck_ref[h, pl.ds(j, 1), lo:lo + width]
            if masked:
                s = jnp.where(visible, s, NEG_BIG)
            m_old = m_scr[h]
            m_new = jnp.maximum(m_old, jnp.max(s, axis=-1, keepdims=True))
            alpha = jnp.exp2(m_old - m_new)
            p = jnp.exp2(s - jnp.tile(m_new, (1, width // LANES)))
            l_scr[h] = alpha * l_scr[h] + jnp.sum(p, axis=-1, keepdims=True)
            acc_scr[h] = alpha * acc_scr[h] + _dot(p.astype(BF16), v_ref[pl.ds(start, width), sl])
            m_scr[h] = m_new

    n_full = (i * tq) // tk

    def pair_step(jj, c):
        block(2 * jj, False)
        block(2 * jj + 1, False)
        return c

    lax.fori_loop(0, n_full // 2, pair_step, 0)

    @pl.when(n_full % 2 == 1)
    def _():
        block(n_full - 1, False)

    if tk == 2 * tq:
        pl.when(i % 2 == 0)(lambda: block(n_full, True, 0, tq))

        @pl.when(i % 2 == 1)
        def _():
            block(n_full, False, 0, tq)
            block(n_full, True, tq, tq)
    else:
        block(n_full, True)
    for h in range(heads):
        sl = slice(h * hd, (h + 1) * hd)
        o = acc_scr[h] / l_scr[h]
        o_ref[:, sl] = (o * gate_ref[:, sl].astype(F32)).astype(o_ref.dtype)


def _fox_attention(qk, v, gate, ck_rows, *, tq=512, tk=FOX_KEY_BLOCK, heads=2):
    B, S, D = v.shape
    H = D // FOX_HEAD_DIM
    HG = H // heads
    W = heads * FOX_HEAD_DIM
    tk = min(tk, S)
    tq = min(tq, tk)
    assert tk in (tq, 2 * tq) and S % tk == 0 and H % heads == 0
    body = functools.partial(_fox_attn_body, tq=tq, tk=tk, heads=heads)
    return pl.pallas_call(
        body,
        grid=(B, HG, S // tq),
        in_specs=[pl.BlockSpec((None, tq, W), lambda b, h, i: (b, i, h)),
                  pl.BlockSpec((None, S, W), lambda b, h, i: (b, 0, HG + h)),
                  pl.BlockSpec((None, S, W), lambda b, h, i: (b, 0, h)),
                  pl.BlockSpec((None, tq, W), lambda b, h, i: (b, i, h)),
                  pl.BlockSpec((None, heads, S // tk, tk), lambda b, h, i: (b, h, 0, 0))],
        out_specs=pl.BlockSpec((None, tq, W), lambda b, h, i: (b, i, h)),
        out_shape=jax.ShapeDtypeStruct((B, S, D), BF16),
        scratch_shapes=[pltpu.VMEM((heads, tq, LANES), F32), pltpu.VMEM((heads, tq, LANES), F32),
                        pltpu.VMEM((heads, tq, FOX_HEAD_DIM), F32)],
        compiler_params=_cparams("parallel", "parallel", "arbitrary"),
        name="fox_attention",
    )(qk, qk, v, gate, ck_rows)


def _cross_attn_body(q_ref, k_ref, v_ref, o_ref, *, heads):
    dh = q_ref.shape[-1] // heads
    for h in range(heads):
        sl = slice(h * dh, (h + 1) * dh)
        s = _dot_nt(q_ref[:, sl], k_ref[:, sl])
        s = s - jnp.max(s, axis=-1, keepdims=True)
        p = jnp.exp(s)
        l = jnp.sum(p, axis=-1, keepdims=True)
        o = _dot(p.astype(BF16), v_ref[:, sl])
        o_ref[:, sl] = (o / l).astype(o_ref.dtype)


def _cross_attention(q, kv, *, tq=512):
    B, S, D = q.shape
    Mem = kv.shape[1]
    tq = min(tq, S)
    body = functools.partial(_cross_attn_body, heads=MEM_HEADS)
    return pl.pallas_call(
        body,
        grid=(B, S // tq),
        in_specs=[pl.BlockSpec((None, tq, D), lambda b, i: (b, i, 0)),
                  pl.BlockSpec((None, Mem, D), lambda b, i: (b, 0, 0)),
                  pl.BlockSpec((None, Mem, D), lambda b, i: (b, 0, 1))],
        out_specs=pl.BlockSpec((None, tq, D), lambda b, i: (b, i, 0)),
        out_shape=jax.ShapeDtypeStruct((B, S, D), BF16),
        compiler_params=_cparams("parallel", "parallel"),
        name="cross_attention",
    )(q, kv, kv)


def _conformer_body(z_ref, zp_ref, w_ref, b_ref, g_ref, beta_ref, o_ref, zs, ys, *, tm, width):
    i = pl.program_id(1)
    D = z_ref.shape[-1]
    n_slabs = D // LANES
    P = CONV_PHASES
    base = CONV_HALO - (width - 1)
    first = i == 0
    for c in range(n_slabs):
        sl = slice(c * LANES, (c + 1) * LANES)
        zs[c, pl.ds(CONV_HALO, tm), :] = z_ref[:, sl]
        zs[c, pl.ds(0, CONV_HALO), :] = jnp.where(first, 0.0, zp_ref[:, sl])
    for c in range(n_slabs):
        sl = slice(c * LANES, (c + 1) * LANES)
        for p in range(P):
            acc = jnp.broadcast_to(b_ref[:, sl], (tm // P, LANES))
            for k in range(width):
                acc = acc + w_ref[k:k + 1, sl] * zs[c, pl.ds(base + k + p, tm // P, stride=P), :]
            ys[c, pl.ds(p, tm // P, stride=P), :] = acc
    total = ys[0]
    for c in range(1, n_slabs):
        total = total + ys[c]
    mu = jnp.sum(total, axis=-1, keepdims=True) * (1.0 / D)
    sq = jnp.zeros((tm, LANES), F32)
    for c in range(n_slabs):
        d = ys[c] - mu
        sq = sq + d * d
    inv = lax.rsqrt(jnp.sum(sq, axis=-1, keepdims=True) * (1.0 / D) + LN_EPS)
    for c in range(n_slabs):
        sl = slice(c * LANES, (c + 1) * LANES)
        t = (ys[c] - mu) * inv * g_ref[:, sl] + beta_ref[:, sl]
        o_ref[:, sl] = (t * _sigmoid(t)).astype(o_ref.dtype)


def _conformer_conv(z, dw_w, dw_b, ln_g, ln_b, *, tm=256):
    B, S, D = z.shape
    width = dw_w.shape[0]
    tm = min(tm, S)
    assert width - 1 <= CONV_HALO and tm % CONV_HALO == 0 and S % tm == 0
    r = tm // CONV_HALO
    body = functools.partial(_conformer_body, tm=tm, width=width)
    row = lambda a: a.reshape(1, D)
    return pl.pallas_call(
        body,
        grid=(B, S // tm),
        in_specs=[pl.BlockSpec((None, tm, D), lambda b, i: (b, i, 0)),
                  pl.BlockSpec((None, CONV_HALO, D), lambda b, i: (b, jnp.maximum(i * r - 1, 0), 0)),
                  pl.BlockSpec((width, D), lambda b, i: (0, 0)),
                  pl.BlockSpec((1, D), lambda b, i: (0, 0)),
                  pl.BlockSpec((1, D), lambda b, i: (0, 0)),
                  pl.BlockSpec((1, D), lambda b, i: (0, 0))],
        out_specs=pl.BlockSpec((None, tm, D), lambda b, i: (b, i, 0)),
        out_shape=jax.ShapeDtypeStruct((B, S, D), BF16),
        scratch_shapes=[pltpu.VMEM((D // LANES, tm + CONV_HALO, LANES), F32),
                        pltpu.VMEM((D // LANES, tm, LANES), F32)],
        compiler_params=_cparams("parallel", "parallel"),
        name="conformer_conv_ln",
    )(z, z, dw_w, row(dw_b), row(ln_g), row(ln_b))


def _head_sum(x, head_dim):
    n = x.shape[-1]
    r = lax.broadcasted_iota(jnp.int32, (LANES, LANES), 0) // head_dim
    c = lax.broadcasted_iota(jnp.int32, (LANES, LANES), 1) // head_dim
    ones = (r == c).astype(BF16)
    parts = []
    for j in range(n // LANES):
        parts.append(_dot_hi_lo_rhs(x[:, j * LANES:(j + 1) * LANES], ones))
    return parts[0] if len(parts) == 1 else jnp.concatenate(parts, axis=-1)


def _rwkv_prep_body(p_ref, w0_ref, db_ref, a0_ref, ib_ref, gb_ref, kk_ref, ka_ref, rk_ref,
                    r_o, k_o, v_o, kk_o, b_o, lw_o, bonus_o, g_o, *, D):
    lw_pad = LANES
    r = p_ref[:, 0:D]
    k = p_ref[:, D:2 * D]
    v = p_ref[:, 2 * D:3 * D]
    o = 3 * D
    w_lo = p_ref[:, o:o + lw_pad]
    a_lo = p_ref[:, o + lw_pad:o + 2 * lw_pad]
    g_lo = p_ref[:, o + 2 * lw_pad:o + 2 * lw_pad + GATE_LORA]

    w_arg = w0_ref[...] + _dot(jnp.tanh(w_lo).astype(BF16), db_ref[...])
    softplus = jnp.maximum(-w_arg, 0.0) + jnp.log(1.0 + jnp.exp(-jnp.abs(w_arg)))
    lw_o[...] = -jnp.exp(-softplus - 0.5)
    a = _sigmoid(a0_ref[...] + _dot(a_lo.astype(BF16), ib_ref[...]))
    g_o[...] = _dot(_sigmoid(g_lo).astype(BF16), gb_ref[...])
    kk = k * kk_ref[...]
    nrm = jnp.sqrt(_head_sum(kk * kk, RWKV_HEAD_DIM))
    kk = kk / jnp.maximum(nrm, 1e-12)
    k2 = k * (1.0 + (a - 1.0) * ka_ref[...])
    r_o[...] = r
    k_o[...] = k2
    v_o[...] = v
    kk_o[...] = kk
    b_o[...] = kk * a
    bonus_o[...] = _head_sum(r * k2 * rk_ref[...], RWKV_HEAD_DIM) * v


def _rwkv_prep(proj, w0, decay_b, a0, iclr_b, gate_b, k_k, k_a, r_k, *, D, tm=256):
    M, P = proj.shape
    tm = min(tm, M)
    body = functools.partial(_rwkv_prep_body, D=D)
    row = lambda a: a.reshape(1, D)
    full = lambda a: pl.BlockSpec(a.shape, lambda m: (0, 0))
    ins = [proj, row(w0), decay_b, row(a0), iclr_b, gate_b, row(k_k), row(k_a), row(r_k)]
    out_spec = pl.BlockSpec((tm, D), lambda m: (m, 0))
    return pl.pallas_call(
        body,
        grid=(M // tm,),
        in_specs=[pl.BlockSpec((tm, P), lambda m: (m, 0))] + [full(a) for a in ins[1:]],
        out_specs=[out_spec] * 8,
        out_shape=[jax.ShapeDtypeStruct((M, D), F32)] * 8,
        compiler_params=_cparams("parallel"),
        name="rwkv_prep",
    )(*ins)


def _rwkv_scan_body(r_ref, k_ref, v_ref, kk_ref, b_ref, lw_ref, bonus_ref, g_ref, lnw_ref, lnb_ref,
                    o_ref, state, *, C, D):
    c_idx = pl.program_id(0)

    @pl.when(c_idx == 0)
    def _():
        state[...] = jnp.zeros_like(state)

    N = RWKV_HEAD_DIM
    GH = RWKV_GROUP_HEADS
    GW = GH * N
    n_groups = D // GW
    n_batch = r_ref.shape[0]
    bf = lambda a: a.astype(BF16)

    row = lax.broadcasted_iota(jnp.int32, (C, GW), 0)
    lane = lax.broadcasted_iota(jnp.int32, (C, GW), 1)
    strict = (lane % N) < row
    incl = (lane % N) <= row
    eye = ((lane % N) == row).astype(F32)
    head_of_lane = lax.broadcasted_iota(jnp.int32, (1, GW), 1) // N
    bd_mask = (lax.broadcasted_iota(jnp.int32, (GW, GW), 0) // N) == (lax.broadcasted_iota(jnp.int32, (GW, GW), 1) // N)
    tri = (lax.broadcasted_iota(jnp.int32, (C, C), 1) <= lax.broadcasted_iota(jnp.int32, (C, C), 0)).astype(BF16)

    def block_diag(x):
        return jnp.concatenate([jnp.where(head_of_lane == h, x, jnp.zeros_like(x)) for h in range(GH)], axis=0)

    units = [(bi, g) for bi in range(n_batch) for g in range(n_groups)]
    gsl = lambda g: slice(g * GW, (g + 1) * GW)

    pre = []
    for bi in range(n_batch):
        lw = lw_ref[bi]
        cum = _dot_exact_lhs(tri, lw)
        cum_last = cum[C - 1:C, :]
        g_inv = jnp.exp(-cum)
        g_tail = jnp.exp(cum_last - cum)
        kf = k_ref[bi]
        bfl = b_ref[bi]
        pre.append(dict(
            r_t=bf(r_ref[bi] * jnp.exp(cum)), kk_t=bf(kk_ref[bi] * jnp.exp(cum - lw)),
            k_t=bf(kf * g_inv), b_t=bf(bfl * g_inv), k_h=bf(kf * g_tail), b_h=bf(bfl * g_tail),
            v=bf(v_ref[bi]), g_last=jnp.exp(cum_last)))

    lhs_kr, Ls, Akks, RKs, Vbd = [], [], [], [], []
    for bi, g in units:
        p = pre[bi]
        lhs = jnp.concatenate([p["kk_t"][:, gsl(g)], p["r_t"][:, gsl(g)]], axis=0)
        ab = _dot_nt(lhs, block_diag(p["b_t"][:, gsl(g)]))
        ak = _dot_nt(lhs, block_diag(p["k_t"][:, gsl(g)]))
        lhs_kr.append(lhs)
        Ls.append(jnp.where(strict, ab[0:C], 0.0))
        Akks.append(bf(jnp.where(strict, ak[0:C], 0.0)))
        RKs.append(jnp.concatenate([bf(jnp.where(incl, ab[C:2 * C], 0.0)),
                                    bf(jnp.where(incl, ak[C:2 * C], 0.0))], axis=1))
        Vbd.append(block_diag(p["v"][:, gsl(g)]))
    AkkVs = [_dot(a, vb) for a, vb in zip(Akks, Vbd)]

    Xs = [eye - L for L in Ls]
    Qs = [_dot(bf(L), block_diag(bf(L))) for L in Ls]
    span = 4
    while span < C:
        nxt = [_dot(jnp.concatenate([bf(Q), bf(X)], axis=0), block_diag(bf(Q))) for Q, X in zip(Qs, Xs)]
        Qs = [n[0:C] for n in nxt]
        Xs = [X + n[C:2 * C] for X, n in zip(Xs, nxt)]
        span *= 2
    Xs = [X + _dot(bf(X), block_diag(bf(Q))) for Q, X in zip(Qs, Xs)]

    KRS = [_dot_nt(lhs, bf(state[bi, g])) for lhs, (bi, g) in zip(lhs_kr, units)]
    Es = [-_dot(bf(X), block_diag(bf(krs[0:C] + akkv))) for X, krs, akkv in zip(Xs, KRS, AkkVs)]
    ys = [[None] * n_groups for _ in range(n_batch)]
    for i, (bi, g) in enumerate(units):
        p = pre[bi]
        e16 = bf(Es[i])
        ys[bi][g] = KRS[i][C:2 * C] + _dot(RKs[i], jnp.concatenate([block_diag(e16), Vbd[i]], axis=0))
        upd = _dot_tn(jnp.concatenate([e16, p["v"][:, gsl(g)]], axis=0),
                      jnp.concatenate([p["b_h"][:, gsl(g)], p["k_h"][:, gsl(g)]], axis=0))
        state[bi, g] = state[bi, g] * p["g_last"][:, gsl(g)] + jnp.where(bd_mask, upd, 0.0)

    for bi in range(n_batch):
        y = jnp.concatenate(ys[bi], axis=-1)
        mean = _head_sum(y, N) * (1.0 / N)
        d = y - mean
        var = _head_sum(d * d, N) * (1.0 / N)
        yn = d * lax.rsqrt(var + RWKV_GN_EPS) * lnw_ref[...] + lnb_ref[...] + bonus_ref[bi]
        o_ref[bi] = (yn * g_ref[bi]).astype(o_ref.dtype)


def _rwkv_scan(r, k, v, kk, b, lw, bonus, g, ln_w, ln_b, *, B, S):
    M, D = r.shape
    C = RWKV_CHUNK
    assert S % C == 0 and C == RWKV_HEAD_DIM and D % (RWKV_GROUP_HEADS * RWKV_HEAD_DIM) == 0
    nc = S // C
    GW = RWKV_GROUP_HEADS * RWKV_HEAD_DIM
    body = functools.partial(_rwkv_scan_body, C=C, D=D)
    tile = pl.BlockSpec((B, C, D), lambda c: (0, c, 0))
    vec = pl.BlockSpec((1, D), lambda c: (0, 0))
    seq = lambda a: a.reshape(B, S, D)
    return pl.pallas_call(
        body,
        grid=(nc,),
        in_specs=[tile] * 8 + [vec, vec],
        out_specs=tile,
        out_shape=jax.ShapeDtypeStruct((B, S, D), BF16),
        scratch_shapes=[pltpu.VMEM((B, D // GW, GW, GW), F32)],
        compiler_params=_cparams("arbitrary"),
        name="rwkv_scan",
    )(*(seq(a) for a in (r, k, v, kk, b, lw, bonus, g)), ln_w.reshape(1, D), ln_b.reshape(1, D)).reshape(M, D)


def _rmsnorm_body(x_ref, g_ref, o_ref):
    o_ref[...] = _rms_rows(x_ref[...], g_ref[...])


def _rmsnorm(x2d, gain_row, *, tm=512):
    M, K = x2d.shape
    tm = min(tm, M)
    return pl.pallas_call(
        _rmsnorm_body,
        grid=(M // tm,),
        in_specs=[pl.BlockSpec((tm, K), lambda m: (m, 0)), pl.BlockSpec((1, K), lambda m: (0, 0))],
        out_specs=pl.BlockSpec((tm, K), lambda m: (m, 0)),
        out_shape=jax.ShapeDtypeStruct((M, K), F32),
        compiler_params=_cparams("parallel"),
        name="final_rmsnorm",
    )(x2d, gain_row)


def _cast_body(w_ref, o_ref):
    o_ref[...] = w_ref[...].astype(o_ref.dtype)


def _to_bf16(w, layer, cols=None):
    _, R, C = w.shape
    cols = cols or C
    tr = R
    while tr * cols * 4 > CAST_BLOCK_BYTES and tr % 32 == 0:
        tr //= 2
    return pl.pallas_call(
        _cast_body,
        grid=(R // tr,),
        in_specs=[pl.BlockSpec((None, tr, cols), lambda r: (layer, r, 0))],
        out_specs=pl.BlockSpec((tr, cols), lambda r: (r, 0)),
        out_shape=jax.ShapeDtypeStruct((R, cols), BF16),
        compiler_params=_cparams("parallel"),
        name="weight_to_bf16",
    )(w)


def _pad_cols(a, n):
    return jnp.pad(a, ((0, 0), (0, n - a.shape[1])))


def _pad_rows(a, n):
    return jnp.pad(a, ((0, n - a.shape[0]), (0, 0)))


def _fox_layer(x2d, B, S, gain_row, w_in, w_forget, b_f, q_gain, k_gain, w_o):
    M, D = x2d.shape
    H = FOX_HEADS
    w_f = _pad_cols(w_forget, LANES).astype(BF16)
    b_row = _pad_cols(b_f.reshape(1, H), LANES)
    scale = FOX_HEAD_DIM ** -0.5
    gains = jnp.concatenate([jnp.tile(q_gain * (scale * LOG2E), H), jnp.tile(k_gain, H)]).reshape(1, 2 * D)
    tn = 1024
    proj = functools.partial(_proj, x2d, gain_row, [w_in], seq=S, out_dtype=BF16, tn=tn)
    qk = proj(epilogue="head_norm", extras=(gains,), w_col_offsets=[0], n_out=2 * D)
    v = proj(epilogue="scale", w_col_offsets=[2 * D // tn], n_out=D)
    gate = proj(epilogue="sigmoid", w_col_offsets=[3 * D // tn], n_out=D)
    c = _forget_cumsum(x2d.reshape(B, S, D), gain_row, w_f, b_row)
    tk = min(FOX_KEY_BLOCK, S)
    ck_rows = c[:, :, :H].transpose(0, 2, 1).reshape(B, H, S // tk, tk)
    o = _fox_attention(qk.reshape(B, S, 2 * D), v.reshape(B, S, D), gate.reshape(B, S, D), ck_rows, tk=tk)
    return _matmul_residual(o.reshape(M, D), w_o, x2d)


def _rwkv_layer(x2d, B, S, gain_row, w_in, mu, w0, decay_b, a0, iclr_b, gate_b, k_k, k_a, r_k,
                ln_w, ln_b, w_o):
    M, D = x2d.shape
    o3 = 3 * D
    segs = [(0, o3, o3), (o3, DECAY_LORA, LANES), (o3 + DECAY_LORA, ICLR_LORA, LANES),
            (o3 + DECAY_LORA + ICLR_LORA, GATE_LORA, GATE_LORA)]
    w_pad = jnp.concatenate([_pad_cols(w_in[:, s:s + n], p) for s, n, p in segs], axis=1)
    mu_pad = jnp.concatenate([_pad_cols(mu[None, s:s + n], p) for s, n, p in segs], axis=1)
    P = w_pad.shape[1]
    tn = 512
    Pp = -(-P // tn) * tn
    w_pad = _pad_cols(w_pad, Pp).astype(BF16)
    mu_pad = _pad_cols(mu_pad, Pp)
    proj = _proj(x2d, gain_row, [w_pad], seq=S, epilogue="lerp", out_dtype=F32, extras=(mu_pad,), tn=tn)
    r, k, v, kk, b, lw, bonus, g = _rwkv_prep(
        proj, w0, _pad_rows(decay_b, LANES).astype(BF16), a0, _pad_rows(iclr_b, LANES).astype(BF16),
        gate_b.astype(BF16), k_k, k_a, r_k.reshape(-1), D=D)
    y = _rwkv_scan(r, k, v, kk, b, lw, bonus, g, ln_w, ln_b, B=B, S=S)
    return _matmul_residual(y, w_o, x2d)


def _conformer_layer(x2d, B, S, gain_row, w_in, dw_w, dw_b, ln_g, ln_b, w_o):
    M, D = x2d.shape
    tn = 512
    z = _proj(x2d, gain_row, [w_in], seq=S, epilogue="glu", out_dtype=F32,
              w_col_offsets=[0, D // tn], tn=tn)
    t = _conformer_conv(z.reshape(B, S, D), dw_w, dw_b, ln_g, ln_b)
    return _matmul_residual(t.reshape(M, D), w_o, x2d)


def _cross_layer(x2d, B, S, gain_row, kv, w_q, w_o):
    M, D = x2d.shape
    scale = (D // MEM_HEADS) ** -0.5
    q = _proj(x2d, gain_row, [w_q], seq=S, epilogue="scale", out_dtype=BF16, scale=scale, tn=1024)
    o = _cross_attention(q.reshape(B, S, D), kv)
    return _matmul_residual(o.reshape(M, D), w_o, x2d)


def _ffn_layer(x2d, S, gain_row, w_up, dw_w, dw_b, w_down):
    d_ff = w_down.shape[0]
    tn = 512
    half = jnp.concatenate([jnp.ones((d_ff,), F32), jnp.full((d_ff,), 0.5, F32)])
    act = _proj(x2d, gain_row, [w_up], seq=S, epilogue="ffn", out_dtype=BF16,
                extras=(dw_w * half, (dw_b * half).reshape(1, -1)), w_col_offsets=[0, d_ff // tn], tn=tn)
    return _matmul_residual(act, w_down, x2d)


def kernel(x, mem, mem_norm, norm_mix, norm_cross, norm_ffn, final_norm, fox_w_in, fox_b_f, fox_q_gain, fox_k_gain, fox_w_o, rwkv_w_in, rwkv_mu, rwkv_w0, rwkv_decay_b, rwkv_a0, rwkv_iclr_b, rwkv_gate_b, rwkv_k_k, rwkv_k_a, rwkv_r_k, rwkv_ln_w, rwkv_ln_b, rwkv_w_o, conv_w_in, conv_dw_w, conv_dw_b, conv_ln_g, conv_ln_b, conv_w_o, cross_w_q, cross_w_kv, cross_w_o, ffn_w_up, ffn_dw_w, ffn_dw_b, ffn_w_down):
    B, S, D = x.shape
    depth = norm_mix.shape[0]
    Mem = mem.shape[1]
    M = B * S
    x2d = x.reshape(M, D)

    mem2d = mem.reshape(B * Mem, D)
    mem_gain = mem_norm.reshape(1, D)

    for i in range(depth):
        kind, j = i % 3, i // 3
        g_mix = norm_mix[i].reshape(1, D)
        if kind == 0:
            x2d = _fox_layer(x2d, B, S, g_mix, fox_w_in[j][:, :4 * D].astype(BF16), fox_w_in[j][:, 4 * D:],
                             fox_b_f[j], fox_q_gain[j], fox_k_gain[j], _to_bf16(fox_w_o, j))
        elif kind == 1:
            x2d = _rwkv_layer(x2d, B, S, g_mix, rwkv_w_in[j], rwkv_mu[j], rwkv_w0[j], rwkv_decay_b[j],
                              rwkv_a0[j], rwkv_iclr_b[j], rwkv_gate_b[j], rwkv_k_k[j], rwkv_k_a[j],
                              rwkv_r_k[j], rwkv_ln_w[j], rwkv_ln_b[j], _to_bf16(rwkv_w_o, j))
        else:
            x2d = _conformer_layer(x2d, B, S, g_mix, _to_bf16(conv_w_in, j), conv_dw_w[j], conv_dw_b[j],
                                   conv_ln_g[j], conv_ln_b[j], _to_bf16(conv_w_o, j))
        kv = _proj(mem2d, mem_gain, [_to_bf16(cross_w_kv, i)], seq=Mem, epilogue="scale", out_dtype=BF16,
                   tn=1024).reshape(B, Mem, 2 * D)
        x2d = _cross_layer(x2d, B, S, norm_cross[i].reshape(1, D), kv, _to_bf16(cross_w_q, i),
                           _to_bf16(cross_w_o, i))
        x2d = _ffn_layer(x2d, S, norm_ffn[i].reshape(1, D), _to_bf16(ffn_w_up, i), ffn_dw_w[i], ffn_dw_b[i],
                         _to_bf16(ffn_w_down, i))
    return _rmsnorm(x2d, final_norm.reshape(1, D)).reshape(B, S, D)
```
